```python
import math
import jax
import jax.numpy as jnp
from jax import lax
import numpy as np

D_MODEL = 1024
BATCH = 32
SEQ = 2048
DEPTH = 4

GRID_W = 64
CTX_LEN = 256
Q_BLOCK = 128
ROPE_THETA = 10000.0
EPS = 1e-6

SSM_EXPAND = 2
SSM_D_INNER = SSM_EXPAND * D_MODEL
SSM_HEAD_DIM = 64
SSM_HEADS = SSM_D_INNER // SSM_HEAD_DIM
SSM_GROUPS = 4
SSM_STATE = 128
SSM_CONV = 5
SSM_CHUNK = 128
SSM_BC = SSM_GROUPS * SSM_STATE
SSM_CONV_DIM = SSM_D_INNER + 2 * SSM_BC

GQA_HEAD_DIM = 128
GQA_HEADS = D_MODEL // GQA_HEAD_DIM
GQA_KV_HEADS = 2
GQA_GROUP = GQA_HEADS // GQA_KV_HEADS
GQA_WIDTH = GQA_HEADS * GQA_HEAD_DIM
GQA_KV_WIDTH = GQA_KV_HEADS * GQA_HEAD_DIM

DIFF_HEAD_DIM = 64
DIFF_HEADS = D_MODEL // (2 * DIFF_HEAD_DIM)
DIFF_QK_WIDTH = DIFF_HEADS * 2 * DIFF_HEAD_DIM
DIFF_V_WIDTH = DIFF_HEADS * 2 * DIFF_HEAD_DIM

N_BRANCHES = 3
FFN_HIDDEN = int(math.ceil(8 * D_MODEL / 3 / 256)) * 256

DEEPNORM_ALPHA = (2 * DEPTH) ** 0.25
DEEPNORM_BETA = (8 * DEPTH) ** -0.25

IN_SIZES = (SSM_D_INNER, SSM_CONV_DIM, 2 * SSM_HEADS,
            GQA_WIDTH, GQA_KV_WIDTH, GQA_KV_WIDTH,
            DIFF_QK_WIDTH, DIFF_QK_WIDTH, DIFF_V_WIDTH,
            N_BRANCHES * D_MODEL)
IN_WIDTH = sum(IN_SIZES)
IN_OFFSETS = tuple(int(o) for o in np.cumsum(IN_SIZES)[:-1])

kernel_name = 'hybrid_ssd_gqa_diffattn_dit_trunk'


def layer_norm(x, g, b):
    xf = x.astype(jnp.float32)
    mu = jnp.mean(xf, axis=-1, keepdims=True)
    var = jnp.mean(jnp.square(xf - mu), axis=-1, keepdims=True)
    return ((xf - mu) * lax.rsqrt(var + EPS) * g + b).astype(x.dtype)


def rms_norm(x, g):
    xf = x.astype(jnp.float32)
    return (xf * lax.rsqrt(jnp.mean(xf * xf, axis=-1, keepdims=True) + EPS) * g).astype(x.dtype)


def modulate(x, shift, scale):
    return x * (1 + scale[:, None]) + shift[:, None]


def axial_rope_tables(pos_row, pos_col, head_dim):
    d_axis = head_dim // 2
    inv_freq = ROPE_THETA ** (-jnp.arange(0, d_axis, 2, dtype=jnp.float32) / d_axis)
    ang_r = pos_row.astype(jnp.float32)[:, None] * inv_freq
    ang_c = pos_col.astype(jnp.float32)[:, None] * inv_freq
    ang = jnp.concatenate([ang_r, ang_r, ang_c, ang_c], axis=-1)
    return jnp.cos(ang), jnp.sin(ang)


def rotate_half(u):
    u1, u2 = jnp.split(u, 2, axis=-1)
    return jnp.concatenate([-u2, u1], axis=-1)


def apply_rope(u, cos, sin):
    shape = (1, cos.shape[0]) + (1,) * (u.ndim - 3) + (cos.shape[1],)
    cos = cos.reshape(shape).astype(u.dtype)
    sin = sin.reshape(shape).astype(u.dtype)
    ur, uc = jnp.split(u, 2, axis=-1)
    rot = jnp.concatenate([rotate_half(ur), rotate_half(uc)], axis=-1)
    return u * cos + rot * sin


def depthwise_conv(u, w, bias):
    k, ch = w.shape
    out = lax.conv_general_dilated(
        u, w.astype(u.dtype)[:, None, :], window_strides=(1,),
        padding=[((k - 1) // 2, k // 2)],
        dimension_numbers=('NWC', 'WIO', 'NWC'), feature_group_count=ch)
    return out + bias


def _flip(u):
    return jnp.flip(u, axis=1)


def segsum_exp(a):
    t = a.shape[-1]
    cs = jnp.cumsum(a, axis=-1)
    diff = cs[..., :, None] - cs[..., None, :]
    mask = jnp.tril(jnp.ones((t, t), dtype=bool))
    return jnp.where(mask, jnp.exp(jnp.where(mask, diff, 0.0)), 0.0)


def ssd_scan(xs, dt, a, b, c, h0):
    bsz, seq, nh, hp = xs.shape
    ng, ns = b.shape[2], b.shape[3]
    nr = nh // ng
    nc = seq // SSM_CHUNK
    q = SSM_CHUNK
    xdt = (xs.astype(jnp.float32) * dt[..., None]).reshape(bsz, nc, q, ng, nr, hp)
    adt = jnp.moveaxis((dt * a).reshape(bsz, nc, q, ng, nr), 2, -1)
    bq = b.astype(jnp.float32).reshape(bsz, nc, q, ng, ns)
    cq = c.astype(jnp.float32).reshape(bsz, nc, q, ng, ns)
    a_cs = jnp.cumsum(adt, axis=-1)
    cb = jnp.einsum('bclgn,bcsgn->bcgls', cq, bq)
    y_diag = jnp.einsum('bcgls,bcgrls,bcsgrp->bclgrp', cb, segsum_exp(adt), xdt)
    decay_to_end = jnp.exp(a_cs[..., -1:] - a_cs)
    chunk_states = jnp.einsum('bclgn,bcgrl,bclgrp->bcgrpn', bq, decay_to_end, xdt)
    states = jnp.concatenate(
        [h0.astype(jnp.float32).reshape(bsz, 1, ng, nr, hp, ns), chunk_states], axis=1)
    totals = jnp.moveaxis(a_cs[..., -1], 1, -1)
    totals = jnp.pad(totals, [(0, 0), (0, 0), (0, 0), (1, 0)])
    new_states = jnp.einsum('bgrzc,bcgrpn->bzgrpn', segsum_exp(totals), states)
    states_in, h_final = new_states[:, :-1], new_states[:, -1]
    y_off = jnp.einsum('bclgn,bcgrpn,bcgrl->bclgrp', cq, states_in, jnp.exp(a_cs))
    y = (y_diag + y_off).reshape(bsz, seq, nh, hp).astype(xs.dtype)
    return y, h_final.reshape(bsz, nh, hp, ns)


def ssm_inputs(xbc, dt_raw, p):
    bsz, seq, _ = xbc.shape
    xbc = jax.nn.silu(depthwise_conv(xbc, p['ssm_conv_w'], p['ssm_conv_b']))
    xs, b, c = jnp.split(xbc, [SSM_D_INNER, SSM_D_INNER + SSM_BC], axis=-1)
    dt = jax.nn.softplus(dt_raw.astype(jnp.float32)
                         + p['ssm_dt_bias'].reshape(-1).astype(jnp.float32))
    return (xs.reshape(bsz, seq, SSM_HEADS, SSM_HEAD_DIM),
            b.reshape(bsz, seq, SSM_GROUPS, SSM_STATE),
            c.reshape(bsz, seq, SSM_GROUPS, SSM_STATE),
            dt.reshape(bsz, seq, 2, SSM_HEADS))


def ssm_output(y, xs, z, p):
    bsz, seq = y.shape[:2]
    y = y + p['ssm_d'][:, None] * xs
    y = y.reshape(bsz, seq, SSM_D_INNER) * jax.nn.silu(z)
    y = rms_norm(y.reshape(bsz, seq, SSM_GROUPS, SSM_D_INNER // SSM_GROUPS),
                 p['ssm_norm_w'].reshape(SSM_GROUPS, -1))
    return y.reshape(bsz, seq, SSM_D_INNER)


def map_query_blocks(fn, q):
    bsz, seq = q.shape[:2]
    nb = seq // Q_BLOCK
    qb = jnp.moveaxis(q.reshape((bsz, nb, Q_BLOCK) + q.shape[2:]), 1, 0)
    out = lax.map(fn, qb)
    return jnp.moveaxis(out, 0, 1).reshape((bsz, seq) + out.shape[3:])


def gqa_heads(q, k, v, p, rope):
    bsz, seq, _ = q.shape
    q = rms_norm(q.reshape(bsz, seq, GQA_HEADS, GQA_HEAD_DIM), p['gqa_q_norm'])
    k = rms_norm(k.reshape(bsz, seq, GQA_KV_HEADS, GQA_HEAD_DIM), p['gqa_k_norm'])
    v = v.reshape(bsz, seq, GQA_KV_HEADS, GQA_HEAD_DIM)
    if rope is not None:
        q = apply_rope(q, *rope)
        k = apply_rope(k, *rope)
    return q.reshape(bsz, seq, GQA_KV_HEADS, GQA_GROUP, GQA_HEAD_DIM), k, v


def gqa_attend(q, k, v):
    scale = GQA_HEAD_DIM ** -0.5

    def block(qb):
        s = jnp.einsum('bqgrd,bkgd->bgrqk', qb, k, preferred_element_type=jnp.float32) * scale
        pr = jax.nn.softmax(s, axis=-1).astype(v.dtype)
        return jnp.einsum('bgrqk,bkgd->bqgrd', pr, v)

    out = map_query_blocks(block, q)
    return out.reshape(out.shape[0], out.shape[1], GQA_WIDTH)


def diff_heads(q, k, v, rope):
    bsz, seq, _ = q.shape
    q = q.reshape(bsz, seq, DIFF_HEADS, 2, DIFF_HEAD_DIM)
    k = k.reshape(bsz, seq, DIFF_HEADS, 2, DIFF_HEAD_DIM)
    v = v.reshape(bsz, seq, DIFF_HEADS, 2 * DIFF_HEAD_DIM)
    if rope is not None:
        q = apply_rope(q, *rope)
        k = apply_rope(k, *rope)
    return q, k, v


def diff_attend(q, k, v, lam):
    scale = DIFF_HEAD_DIM ** -0.5

    def block(qb):
        s = jnp.einsum('bqhjd,bkhjd->bhjqk', qb, k, preferred_element_type=jnp.float32) * scale
        pr = jax.nn.softmax(s, axis=-1)
        w = (pr[:, :, 0] - lam * pr[:, :, 1]).astype(v.dtype)
        return jnp.einsum('bhqk,bkhe->bqhe', w, v)

    return map_query_blocks(block, q)


def diff_output(o, p, lam_init):
    bsz, seq = o.shape[:2]
    o = rms_norm(o, p['diff_norm_w']) * (1.0 - lam_init)
    return o.reshape(bsz, seq, DIFF_V_WIDTH)


def merge_branches(y_ssm, y_gqa, y_diff, gate_logits, p):
    bsz, seq = y_ssm.shape[:2]
    g = jax.nn.sigmoid((gate_logits + p['b_gate']).astype(jnp.float32))
    g = g.reshape(bsz, seq, N_BRANCHES, D_MODEL).astype(y_ssm.dtype)
    m = (g[:, :, 0] * (y_ssm @ p['w_ssm_out'])
         + g[:, :, 1] * (y_gqa @ p['w_gqa_out'])
         + g[:, :, 2] * (y_diff @ p['w_diff_out']))
    return m @ p['w_o']


def hybrid_mixer(h_lat, h_ctx, p, rope_gqa, rope_diff, lam_init, need_ctx):
    bsz = h_lat.shape[0]
    (z_l, xbc_l, dt_l, gq_l, gk_l, gv_l, dq_l, dk_l, dv_l, gate_l) = jnp.split(
        h_lat @ p['w_in'], IN_OFFSETS, axis=-1)
    (z_c, xbc_c, dt_c, gq_c, gk_c, gv_c, dq_c, dk_c, dv_c, gate_c) = jnp.split(
        h_ctx @ p['w_in'], IN_OFFSETS, axis=-1)

    xs_l, b_l, c_l, dtp_l = ssm_inputs(xbc_l, dt_l, p)
    xs_c, b_c, c_c, dtp_c = ssm_inputs(xbc_c, dt_c, p)
    a = -jnp.exp(p['ssm_a_log'].astype(jnp.float32))
    h0 = jnp.zeros((bsz, SSM_HEADS, SSM_HEAD_DIM, SSM_STATE), jnp.float32)
    yf_c, hf_c = ssd_scan(xs_c, dtp_c[:, :, 0], a[0], b_c, c_c, h0)
    yf_l, _ = ssd_scan(xs_l, dtp_l[:, :, 0], a[0], b_l, c_l, hf_c)
    yb_c, hb_c = ssd_scan(_flip(xs_c), _flip(dtp_c[:, :, 1]), a[1], _flip(b_c), _flip(c_c), h0)
    yb_l, _ = ssd_scan(_flip(xs_l), _flip(dtp_l[:, :, 1]), a[1], _flip(b_l), _flip(c_l), hb_c)
    ssm_l = ssm_output(yf_l + _flip(yb_l), xs_l, z_l, p)

    q_l, k_l, v_l = gqa_heads(gq_l, gk_l, gv_l, p, rope_gqa)
    q_c, k_c, v_c = gqa_heads(gq_c, gk_c, gv_c, p, None)
    gqa_l = gqa_attend(q_l, jnp.concatenate([k_c, k_l], axis=1),
                       jnp.concatenate([v_c, v_l], axis=1))

    lq1, lk1, lq2, lk2 = p['diff_lambda'].astype(jnp.float32)
    lam = jnp.exp(jnp.sum(lq1 * lk1)) - jnp.exp(jnp.sum(lq2 * lk2)) + lam_init
    dq_l, dk_l, dv_l = diff_heads(dq_l, dk_l, dv_l, rope_diff)
    dq_c, dk_c, dv_c = diff_heads(dq_c, dk_c, dv_c, None)
    diff_l = diff_output(diff_attend(dq_l, jnp.concatenate([dk_c, dk_l], axis=1),
                                     jnp.concatenate([dv_c, dv_l], axis=1), lam), p, lam_init)

    y_lat = merge_branches(ssm_l, gqa_l, diff_l, gate_l, p)
    if not need_ctx:
        return y_lat, None
    ssm_c = ssm_output(yf_c + _flip(yb_c), xs_c, z_c, p)
    gqa_c = gqa_attend(q_c, k_c, v_c)
    diff_c = diff_output(diff_attend(dq_c, dk_c, dv_c, lam), p, lam_init)
    y_ctx = merge_branches(ssm_c, gqa_c, diff_c, gate_c, p)
    return y_lat, y_ctx


def swiglu(h, w_in, w_out):
    a, b = jnp.split(h @ w_in, 2, axis=-1)
    return (jax.nn.silu(a) * b) @ w_out


def setup_inputs(seed: int = 0) -> dict:
    key = jax.random.key(seed)
    keys = iter(jax.random.split(key, 40))

    def nrm(shape, scale):
        return jax.random.normal(next(keys), shape, jnp.float32) * scale

    d = D_MODEL
    x = nrm((BATCH, SEQ, d), 1.0)
    c = nrm((BATCH, d), 1.0)
    ctx = nrm((BATCH, CTX_LEN, d), 1.0)
    c_ctx = nrm((d,), 1.0)
    ada_w = nrm((DEPTH, d, 6 * d), d ** -0.5)
    ada_b = nrm((DEPTH, 6 * d), 0.02)
    w_in = nrm((DEPTH, d, IN_WIDTH), d ** -0.5)
    b_gate = nrm((DEPTH, N_BRANCHES * d), 0.02)
    ssm_conv_w = nrm((DEPTH, SSM_CONV, SSM_CONV_DIM), SSM_CONV ** -0.5)
    ssm_conv_b = nrm((DEPTH, SSM_CONV_DIM), 0.02)
    dt0 = jnp.exp(jax.random.uniform(next(keys), (DEPTH, 2, SSM_HEADS), jnp.float32,
                                     minval=math.log(1e-3), maxval=math.log(1e-1)))
    ssm_dt_bias = dt0 + jnp.log(-jnp.expm1(-dt0))
    ssm_a_log = jnp.log(jax.random.uniform(next(keys), (DEPTH, 2, SSM_HEADS), jnp.float32,
                                           minval=1.0, maxval=16.0))
    ssm_d = 1.0 + nrm((DEPTH, SSM_HEADS), 0.02)
    ssm_norm_w = 1.0 + nrm((DEPTH, SSM_D_INNER), 0.02)
    w_ssm_out = nrm((DEPTH, SSM_D_INNER, d), SSM_D_INNER ** -0.5)
    gqa_q_norm = 1.0 + nrm((DEPTH, GQA_HEAD_DIM), 0.02)
    gqa_k_norm = 1.0 + nrm((DEPTH, GQA_HEAD_DIM), 0.02)
    w_gqa_out = nrm((DEPTH, GQA_WIDTH, d), GQA_WIDTH ** -0.5)
    diff_lambda = nrm((DEPTH, 4, DIFF_HEAD_DIM), 0.1)
    diff_norm_w = 1.0 + nrm((DEPTH, 2 * DIFF_HEAD_DIM), 0.02)
    w_diff_out = nrm((DEPTH, DIFF_V_WIDTH, d), DIFF_V_WIDTH ** -0.5)
    w_o = nrm((DEPTH, d, d), d ** -0.5 * DEEPNORM_BETA)
    ln1_g = 1.0 + nrm((DEPTH, d), 0.02)
    ln1_b = nrm((DEPTH, d), 0.02)
    ffn_w_in = nrm((DEPTH, d, 2 * FFN_HIDDEN), d ** -0.5)
    ffn_w_out = nrm((DEPTH, FFN_HIDDEN, d), FFN_HIDDEN ** -0.5 * DEEPNORM_BETA)
    ln2_g = 1.0 + nrm((DEPTH, d), 0.02)
    ln2_b = nrm((DEPTH, d), 0.02)
    return {'x': x, 'c': c, 'ctx': ctx, 'c_ctx': c_ctx,
            'ada_w': ada_w, 'ada_b': ada_b, 'w_in': w_in, 'b_gate': b_gate,
            'ssm_conv_w': ssm_conv_w, 'ssm_conv_b': ssm_conv_b, 'ssm_dt_bias': ssm_dt_bias,
            'ssm_a_log': ssm_a_log, 'ssm_d': ssm_d, 'ssm_norm_w': ssm_norm_w,
            'w_ssm_out': w_ssm_out, 'gqa_q_norm': gqa_q_norm, 'gqa_k_norm': gqa_k_norm,
            'w_gqa_out': w_gqa_out, 'diff_lambda': diff_lambda, 'diff_norm_w': diff_norm_w,
            'w_diff_out': w_diff_out, 'w_o': w_o, 'ln1_g': ln1_g, 'ln1_b': ln1_b,
            'ffn_w_in': ffn_w_in, 'ffn_w_out': ffn_w_out, 'ln2_g': ln2_g, 'ln2_b': ln2_b}


def reference(x, c, ctx, c_ctx, ada_w, ada_b, w_in, b_gate, ssm_conv_w, ssm_conv_b,
              ssm_dt_bias, ssm_a_log, ssm_d, ssm_norm_w, w_ssm_out, gqa_q_norm, gqa_k_norm,
              w_gqa_out, diff_lambda, diff_norm_w, w_diff_out, w_o, ln1_g, ln1_b,
              ffn_w_in, ffn_w_out, ln2_g, ln2_b):
    seq = x.shape[1]
    rows = seq // GRID_W
    t = jnp.arange(rows * GRID_W, dtype=jnp.int32)
    pos_row = t // GRID_W
    pos_col = t % GRID_W
    rope_gqa = axial_rope_tables(pos_row, pos_col, GQA_HEAD_DIM)
    rope_diff = axial_rope_tables(pos_row, pos_col, DIFF_HEAD_DIM)
    sc = jax.nn.silu(c)
    sc_ctx = jax.nn.silu(c_ctx)[None]
    for i in range(DEPTH):
        need_ctx = i < DEPTH - 1
        lam_init = 0.8 - 0.6 * math.exp(-0.3 * i)
        p = {'w_in': w_in[i], 'b_gate': b_gate[i], 'ssm_conv_w': ssm_conv_w[i],
             'ssm_conv_b': ssm_conv_b[i], 'ssm_dt_bias': ssm_dt_bias[i], 'ssm_a_log': ssm_a_log[i],
             'ssm_d': ssm_d[i], 'ssm_norm_w': ssm_norm_w[i], 'w_ssm_out': w_ssm_out[i],
             'gqa_q_norm': gqa_q_norm[i], 'gqa_k_norm': gqa_k_norm[i], 'w_gqa_out': w_gqa_out[i],
             'diff_lambda': diff_lambda[i], 'diff_norm_w': diff_norm_w[i],
             'w_diff_out': w_diff_out[i], 'w_o': w_o[i]}
        mod_l = jnp.split(sc @ ada_w[i] + ada_b[i], 6, axis=-1)
        mod_c = jnp.split(sc_ctx @ ada_w[i] + ada_b[i], 6, axis=-1)
        y_l, y_c = hybrid_mixer(modulate(x, mod_l[0], mod_l[1]),
                                modulate(ctx, mod_c[0], mod_c[1]),
                                p, rope_gqa, rope_diff, lam_init, need_ctx)
        x = layer_norm(DEEPNORM_ALPHA * x + mod_l[2][:, None] * y_l, ln1_g[i], ln1_b[i])
        x = layer_norm(DEEPNORM_ALPHA * x + mod_l[5][:, None]
                       * swiglu(modulate(x, mod_l[3], mod_l[4]), ffn_w_in[i], ffn_w_out[i]),
                       ln2_g[i], ln2_b[i])
        if need_ctx:
            ctx = layer_norm(DEEPNORM_ALPHA * ctx + mod_c[2][:, None] * y_c, ln1_g[i], ln1_b[i])
            ctx = layer_norm(DEEPNORM_ALPHA * ctx + mod_c[5][:, None]
                             * swiglu(modulate(ctx, mod_c[3], mod_c[4]), ffn_w_in[i], ffn_w_out[i]),
                             ln2_g[i], ln2_b[i])
    return x
```

```python
import functools
import math

import jax
import jax.numpy as jnp
from jax import lax
from jax.experimental import pallas as pl
from jax.experimental.pallas import tpu as pltpu

F32 = jnp.float32
BF16 = jnp.bfloat16

D_MODEL = 1024
DEPTH = 4
GRID_W = 64
ROPE_THETA = 10000.0
EPS = 1e-6

SSM_D_INNER = 2048
SSM_HEAD_DIM = 64
SSM_HEADS = 32
SSM_GROUPS = 4
SSM_STATE = 128
SSM_CONV = 5
SSM_CHUNK = 128
SSM_BC = SSM_GROUPS * SSM_STATE
SSM_CONV_DIM = SSM_D_INNER + 2 * SSM_BC
SSM_GROUP_WIDTH = SSM_D_INNER // SSM_GROUPS
SSM_GROUP_HEADS = SSM_HEADS // SSM_GROUPS

GQA_HEAD_DIM = 128
GQA_HEADS = 8
GQA_KV_HEADS = 2
GQA_GROUP = GQA_HEADS // GQA_KV_HEADS
GQA_WIDTH = GQA_HEADS * GQA_HEAD_DIM
GQA_KV_WIDTH = GQA_KV_HEADS * GQA_HEAD_DIM

DIFF_HEAD_DIM = 64
DIFF_HEADS = 8
DIFF_WIDTH = DIFF_HEADS * 2 * DIFF_HEAD_DIM

N_BRANCHES = 3
FFN_HIDDEN = 2816
FFN_CHUNK = 1408

DEEPNORM_ALPHA = (2 * DEPTH) ** 0.25

LANES = 128
SUBLANES = 8
DT_WIDTH = LANES
VMEM_LIMIT = 56 * 1024 * 1024

OFF_Z = 0
OFF_XBC = OFF_Z + SSM_D_INNER
OFF_DT = OFF_XBC + SSM_CONV_DIM
OFF_GQ = OFF_DT + 2 * SSM_HEADS
OFF_GK = OFF_GQ + GQA_WIDTH
OFF_GV = OFF_GK + GQA_KV_WIDTH
OFF_DQ = OFF_GV + GQA_KV_WIDTH
OFF_DK = OFF_DQ + DIFF_WIDTH
OFF_DV = OFF_DK + DIFF_WIDTH
OFF_GATE = OFF_DV + DIFF_WIDTH
IN_WIDTH = OFF_GATE + N_BRANCHES * D_MODEL


def _dot(a, b):
    return jnp.dot(a, b, preferred_element_type=F32)


def _dot_nt(a, b):
    return lax.dot_general(a, b, (((1,), (1,)), ((), ())), preferred_element_type=F32)


def _split3(a):
    hi = a.astype(BF16)
    r = a - hi.astype(F32)
    mid = r.astype(BF16)
    lo = (r - mid.astype(F32)).astype(BF16)
    return hi, mid, lo


def _dot_exact_rhs(a, sel):
    hi, mid, lo = _split3(a)
    return _dot(hi, sel) + _dot(mid, sel) + _dot(lo, sel)


def _dot_exact_lhs(sel, a):
    hi, mid, lo = _split3(a)
    return _dot(sel, hi) + _dot(sel, mid) + _dot(sel, lo)


def _sigmoid(x):
    return 1.0 / (1.0 + jnp.exp(-x))


def _silu(x):
    return x * _sigmoid(x)


def _layer_norm(r, g, b):
    mu = jnp.mean(r, axis=-1, keepdims=True)
    d = r - mu
    var = jnp.mean(d * d, axis=-1, keepdims=True)
    return d * lax.rsqrt(var + EPS) * g + b


def _params(*sem):
    return pltpu.CompilerParams(dimension_semantics=sem, vmem_limit_bytes=VMEM_LIMIT)


def _const_spec(shape):
    nd = len(shape)
    return pl.BlockSpec(shape, lambda *_: (0,) * nd, pipeline_mode=pl.Buffered(1))


def _ada_kernel(c_ref, w_ref, b_ref, o_ref):
    sc = _silu(c_ref[...]).astype(BF16)
    o_ref[...] = _dot(sc, w_ref[...].astype(BF16)) + b_ref[...]


def _ada_call(cvec, ada_w, ada_b):
    rows = cvec.shape[0]
    depth, d, n = ada_w.shape
    tn = n // 4
    return pl.pallas_call(
        _ada_kernel,
        grid=(depth, n // tn),
        in_specs=[
            pl.BlockSpec((rows, d), lambda i, j: (0, 0)),
            pl.BlockSpec((None, d, tn), lambda i, j: (i, 0, j)),
            pl.BlockSpec((None, 1, tn), lambda i, j: (i, 0, j)),
        ],
        out_specs=pl.BlockSpec((None, rows, tn), lambda i, j: (i, 0, j)),
        out_shape=jax.ShapeDtypeStruct((depth, rows, n), F32),
        compiler_params=_params("arbitrary", "arbitrary"),
    )(cvec, ada_w, ada_b.reshape(depth, 1, n))


def _modulated(x_ref, mod_ref, shift_row):
    x = x_ref[...]
    shift = mod_ref[shift_row:shift_row + 1, :]
    scale = mod_ref[shift_row + 1:shift_row + 2, :]
    return (x * (1.0 + scale) + shift).astype(BF16)


def _inproj_ssm_kernel(x_ref, mod_ref, w_ref, z_ref, xbc_ref, dt_ref):
    h = _modulated(x_ref, mod_ref, 0)
    step = 512
    for c0 in range(0, SSM_D_INNER, step):
        z_ref[:, c0:c0 + step] = _dot(h, w_ref[:, OFF_Z + c0:OFF_Z + c0 + step])
    for c0 in range(0, SSM_CONV_DIM, step):
        xbc_ref[:, c0:c0 + step] = _dot(h, w_ref[:, OFF_XBC + c0:OFF_XBC + c0 + step])
    dt_ref[...] = _dot(h, w_ref[:, OFF_DT:OFF_DT + DT_WIDTH])


def _mod_index(nct):
    return lambda b, t: (2 * b + jnp.where(t >= nct, 1, 0), 0, 0)


def _inproj_ssm_call(xs, mods, w_ssm, ctx_len, tm):
    bsz, seq, d = xs.shape
    nct = ctx_len // tm
    tok = lambda w: pl.BlockSpec((None, tm, w), lambda b, t: (b, t, 0))
    return pl.pallas_call(
        _inproj_ssm_kernel,
        grid=(bsz, seq // tm),
        in_specs=[tok(d), pl.BlockSpec((None, 6, d), _mod_index(nct)), _const_spec(w_ssm.shape)],
        out_specs=[tok(SSM_D_INNER), tok(SSM_CONV_DIM), tok(DT_WIDTH)],
        out_shape=[jax.ShapeDtypeStruct((bsz, seq, SSM_D_INNER), F32),
                   jax.ShapeDtypeStruct((bsz, seq, SSM_CONV_DIM), F32),
                   jax.ShapeDtypeStruct((bsz, seq, DT_WIDTH), F32)],
        compiler_params=_params("parallel", "parallel"),
    )(xs, mods, w_ssm)


def _rope(u, cos, sin_signed, half):
    lane = lax.broadcasted_iota(jnp.int32, u.shape, 1)
    fwd = pltpu.roll(u, LANES - half, axis=1)
    bwd = pltpu.roll(u, half, axis=1)
    partner = jnp.where((lane % (2 * half)) < half, fwd, bwd)
    return u * cos + partner * sin_signed


def _rms_heads(u, g):
    ms = jnp.mean(u * u, axis=-1, keepdims=True)
    return u * lax.rsqrt(ms + EPS) * g


def _inproj_attn_kernel(x_ref, mod_ref, w_ref, qn_ref, kn_ref, cg_ref, sg_ref, cd_ref, sd_ref,
                        gq_ref, gk_ref, gv_ref, dq_ref, dk_ref, dv_ref):
    h = _modulated(x_ref, mod_ref, 0)
    base = OFF_GQ
    cg, sg = cg_ref[...], sg_ref[...]
    cd, sd = cd_ref[...], sd_ref[...]
    hd = GQA_HEAD_DIM
    for i in range(GQA_HEADS):
        u = _dot(h, w_ref[:, OFF_GQ - base + i * hd:OFF_GQ - base + (i + 1) * hd])
        u = _rope(_rms_heads(u, qn_ref[...]), cg, sg, GQA_HEAD_DIM // 4)
        gq_ref[:, i * hd:(i + 1) * hd] = u.astype(BF16)
    for i in range(GQA_KV_HEADS):
        u = _dot(h, w_ref[:, OFF_GK - base + i * hd:OFF_GK - base + (i + 1) * hd])
        u = _rope(_rms_heads(u, kn_ref[...]), cg, sg, GQA_HEAD_DIM // 4)
        gk_ref[:, i * hd:(i + 1) * hd] = u.astype(BF16)
    gv_ref[...] = _dot(h, w_ref[:, OFF_GV - base:OFF_GV - base + GQA_KV_WIDTH]).astype(BF16)
    scale = DIFF_HEAD_DIM ** -0.5
    for i in range(DIFF_HEADS):
        u = _dot(h, w_ref[:, OFF_DQ - base + i * LANES:OFF_DQ - base + (i + 1) * LANES])
        dq_ref[:, i * LANES:(i + 1) * LANES] = (
            _rope(u, cd, sd, DIFF_HEAD_DIM // 4) * scale).astype(BF16)
    for i in range(DIFF_HEADS):
        u = _dot(h, w_ref[:, OFF_DK - base + i * LANES:OFF_DK - base + (i + 1) * LANES])
        dk_ref[:, i * LANES:(i + 1) * LANES] = _rope(u, cd, sd, DIFF_HEAD_DIM // 4).astype(BF16)
    for c0 in range(0, DIFF_WIDTH, 512):
        dv_ref[:, c0:c0 + 512] = _dot(
            h, w_ref[:, OFF_DV - base + c0:OFF_DV - base + c0 + 512]).astype(BF16)


def _inproj_attn_call(xs, mods, w_attn, qn, kn, tabs, ctx_len, tm):
    bsz, seq, d = xs.shape
    nct = ctx_len // tm
    tok = lambda w: pl.BlockSpec((None, tm, w), lambda b, t: (b, t, 0))
    tab = pl.BlockSpec((tm, LANES), lambda b, t: (t, 0))
    widths = [GQA_WIDTH, GQA_KV_WIDTH, GQA_KV_WIDTH, DIFF_WIDTH, DIFF_WIDTH, DIFF_WIDTH]
    return pl.pallas_call(
        _inproj_attn_kernel,
        grid=(bsz, seq // tm),
        in_specs=[tok(d), pl.BlockSpec((None, 6, d), _mod_index(nct)), _const_spec(w_attn.shape),
                  _const_spec(qn.shape), _const_spec(kn.shape), tab, tab, tab, tab],
        out_specs=[tok(w) for w in widths],
        out_shape=[jax.ShapeDtypeStruct((bsz, seq, w), BF16) for w in widths],
        compiler_params=_params("parallel", "parallel"),
    )(xs, mods, w_attn, qn, kn, *tabs)


def _ssd_kernel(xr_ref, br_ref, cr_ref, dtr_ref, z_ref, cwx_ref, cwb_ref, cwc_ref,
                dtb_ref, alog_ref, dskip_ref, nw_ref, o_ref,
                xs_s, b_s, bt_s, c_s, dt_s, acs_s, acst_s, yf_s, st_s, *, n_chunks, n_ctx_chunks):
    q = SSM_CHUNK
    gh = SSM_GROUP_HEADS
    gw = SSM_GROUP_WIDTH
    g = pl.program_id(1)

    row_i = lax.broadcasted_iota(jnp.int32, (q, q), 0)
    col_i = lax.broadcasted_iota(jnp.int32, (q, q), 1)
    tri_lo = (col_i <= row_i).astype(BF16)
    tri_up = (col_i >= row_i).astype(BF16)
    src = jnp.where(col_i < gh, g * gh + col_i, SSM_HEADS + g * gh + col_i - gh)
    pick = ((row_i == src) & (col_i < 2 * gh)).astype(BF16)
    a_neg = -jnp.exp(alog_ref[...])
    dt_bias = dtb_ref[...]

    def conv_silu(ref, cw_ref, c):
        width = ref.shape[1]
        r0 = c * q
        first = c == 0
        last = c == n_chunks - 1
        pad = jnp.zeros((SUBLANES, width), F32)
        parts = [pad if first else ref[r0 - SUBLANES:r0, :], ref[r0:r0 + q, :],
                 pad if last else ref[r0 + q:r0 + q + SUBLANES, :]]
        slab = jnp.concatenate(parts, axis=0)
        rows = q + 2 * SUBLANES
        ridx = lax.broadcasted_iota(jnp.int32, slab.shape, 0)
        if c == n_ctx_chunks:
            slab = jnp.where(ridx < SUBLANES, 0.0, slab)
        if c == n_ctx_chunks - 1:
            slab = jnp.where(ridx >= q + SUBLANES, 0.0, slab)
        acc = None
        for k in range(SSM_CONV):
            sh = (SSM_CONV // 2 - k) % rows
            t = slab if sh == 0 else pltpu.roll(slab, sh, axis=0)
            term = t[SUBLANES:SUBLANES + q, :] * cw_ref[k:k + 1, :]
            acc = term if acc is None else acc + term
        return _silu(acc + cw_ref[SSM_CONV:SSM_CONV + 1, :])

    for c in range(n_chunks):
        r0 = c * q
        xs_s[r0:r0 + q, :] = conv_silu(xr_ref, cwx_ref, c)
        bc = conv_silu(br_ref, cwb_ref, c)
        b_s[r0:r0 + q, :] = bc.astype(BF16)
        bt_s[c] = bc.T.astype(BF16)
        c_s[r0:r0 + q, :] = conv_silu(cr_ref, cwc_ref, c).astype(BF16)
        raw = _dot_exact_rhs(dtr_ref[r0:r0 + q, :], pick) + dt_bias
        dtv = jnp.maximum(raw, 0.0) + jnp.log1p(jnp.exp(-jnp.abs(raw)))
        adt = dtv * a_neg
        acs = jnp.where(col_i < gh, _dot_exact_lhs(tri_lo, adt), _dot_exact_lhs(tri_up, adt))
        dt_s[r0:r0 + q, :] = dtv
        acs_s[r0:r0 + q, :] = acs
        acst_s[c] = acs.T

    lane_g = lax.broadcasted_iota(jnp.int32, (q, gw), 1)
    row_g = lax.broadcasted_iota(jnp.int32, (q, gw), 0)
    lane_q = lax.broadcasted_iota(jnp.int32, (q, LANES), 1)

    def scan_chunk(c, direction):
        r0 = pl.multiple_of(c * q, q)
        rows = pl.ds(r0, q)
        xs = xs_s[rows, :]
        bc = b_s[rows, :]
        cc = c_s[rows, :]
        dtv = dt_s[rows, :]
        acs = acs_s[rows, :]
        acst = acst_s[c]
        expand = (row_g == gh * direction + lane_g // SSM_HEAD_DIM).astype(BF16)
        end = q - 1 if direction == 0 else 0
        tot = acs[end:end + 1, :]
        mine = (col_i >= gh * direction) & (col_i < gh * (direction + 1))
        to_end = jnp.exp(jnp.where(mine, tot - acs, 0.0))
        stacked = jnp.concatenate([dtv, dtv * to_end, jnp.exp(acs)], axis=0)
        ex = _dot_exact_rhs(stacked, expand)
        xdt = (xs * ex[0:q]).astype(BF16)
        xdec = (xs * ex[q:2 * q]).astype(BF16)
        e_in = ex[2 * q:3 * q]
        e_tot = e_in[end:end + 1, :]
        cb = _dot_nt(cc, bc)
        keep = (row_i >= col_i) if direction == 0 else (col_i >= row_i)
        outs = []
        for pair in range(gh // 2):
            ms = []
            for hh in range(2):
                col = gh * direction + 2 * pair + hh
                diff = acs[:, col:col + 1] - acst[col:col + 1, :]
                decay = jnp.where(keep, jnp.exp(jnp.where(keep, diff, 0.0)), 0.0)
                ms.append((cb * decay).astype(BF16))
            lhs = jnp.concatenate(ms, axis=1)
            p = xdt[:, pair * LANES:(pair + 1) * LANES]
            zero = jnp.zeros_like(p)
            rhs = jnp.concatenate([jnp.where(lane_q < SSM_HEAD_DIM, p, zero),
                                   jnp.where(lane_q >= SSM_HEAD_DIM, p, zero)], axis=0)
            outs.append(_dot(lhs, rhs))
        y = jnp.concatenate(outs, axis=1)
        st = st_s[...]
        y = y + _dot(cc, st.astype(BF16)) * e_in
        st_s[...] = st * e_tot + _dot(bt_s[c], xdec)
        return rows, xs, y

    st_s[...] = jnp.zeros_like(st_s)

    def fwd_body(c, carry):
        rows, _, y = scan_chunk(c, 0)
        yf_s[rows, :] = y
        return carry

    lax.fori_loop(0, n_chunks, fwd_body, 0)

    st_s[...] = jnp.zeros_like(st_s)

    def bwd_body(i, carry):
        c = jnp.where(i < n_ctx_chunks, n_ctx_chunks - 1 - i, n_chunks - 1 - (i - n_ctx_chunks))
        rows, xs, yb = scan_chunk(c, 1)
        y = yf_s[rows, :] + yb + dskip_ref[...] * xs
        y = y * _silu(z_ref[rows, :])
        ms = jnp.mean(y * y, axis=-1, keepdims=True)
        o_ref[rows, :] = (y * lax.rsqrt(ms + EPS) * nw_ref[...]).astype(BF16)
        return carry

    lax.fori_loop(0, n_chunks, bwd_body, 0)


def _ssd_call(xbc, dt, z, sp, ctx_len):
    bsz, seq, _ = xbc.shape
    q, gw, ng = SSM_CHUNK, SSM_GROUP_WIDTH, SSM_GROUPS
    n_chunks = seq // q
    xb = SSM_D_INNER // SSM_STATE
    cb = (SSM_D_INNER + SSM_BC) // SSM_STATE
    grp = lambda r, w: pl.BlockSpec((None, r, w), lambda b, g: (g, 0, 0))
    kern = functools.partial(_ssd_kernel, n_chunks=n_chunks, n_ctx_chunks=ctx_len // q)
    return pl.pallas_call(
        kern,
        grid=(bsz, ng),
        in_specs=[
            pl.BlockSpec((None, seq, gw), lambda b, g: (b, 0, g)),
            pl.BlockSpec((None, seq, SSM_STATE), lambda b, g: (b, 0, xb + g)),
            pl.BlockSpec((None, seq, SSM_STATE), lambda b, g: (b, 0, cb + g)),
            pl.BlockSpec((None, seq, DT_WIDTH), lambda b, g: (b, 0, 0)),
            pl.BlockSpec((None, seq, gw), lambda b, g: (b, 0, g)),
            grp(SUBLANES, gw), grp(SUBLANES, SSM_STATE), grp(SUBLANES, SSM_STATE),
            grp(1, DT_WIDTH), grp(1, DT_WIDTH), grp(1, gw), grp(1, gw),
        ],
        out_specs=pl.BlockSpec((None, seq, gw), lambda b, g: (b, 0, g)),
        out_shape=jax.ShapeDtypeStruct((bsz, seq, SSM_D_INNER), BF16),
        scratch_shapes=[
            pltpu.VMEM((seq, gw), F32),
            pltpu.VMEM((seq, SSM_STATE), BF16),
            pltpu.VMEM((n_chunks, SSM_STATE, q), BF16),
            pltpu.VMEM((seq, SSM_STATE), BF16),
            pltpu.VMEM((seq, DT_WIDTH), F32),
            pltpu.VMEM((seq, DT_WIDTH), F32),
            pltpu.VMEM((n_chunks, DT_WIDTH, q), F32),
            pltpu.VMEM((seq, gw), F32),
            pltpu.VMEM((SSM_STATE, gw), F32),
        ],
        compiler_params=_params("parallel", "arbitrary"),
    )(xbc, xbc, xbc, dt, z, sp["cwx"], sp["cwb"], sp["cwc"], sp["dtb"], sp["alog"],
      sp["dskip"], sp["nw"])


def _gqa_kernel(q_ref, k_ref, v_ref, o_ref, *, n_ctx_tiles, ctx_len):
    scale = GQA_HEAD_DIM ** -0.5

    def attend(nk):
        k = k_ref[0:nk, :]
        v = v_ref[0:nk, :]
        for r in range(GQA_GROUP):
            cols = slice(r * GQA_HEAD_DIM, (r + 1) * GQA_HEAD_DIM)
            s = _dot_nt(q_ref[:, cols], k) * scale
            m = jnp.max(s, axis=-1, keepdims=True)
            p = jnp.exp(s - m)
            l = jnp.sum(p, axis=-1, keepdims=True)
            o_ref[:, cols] = (_dot(p.astype(BF16), v) / l).astype(BF16)

    t = pl.program_id(2)

    @pl.when(t < n_ctx_tiles)
    def _():
        attend(ctx_len)

    @pl.when(t >= n_ctx_tiles)
    def _():
        attend(k_ref.shape[0])


def _gqa_call(q, k, v, ctx_len, tq, row_off):
    bsz, seq, _ = q.shape
    gwid = GQA_GROUP * GQA_HEAD_DIM
    kern = functools.partial(_gqa_kernel, n_ctx_tiles=ctx_len // tq - row_off, ctx_len=ctx_len)
    kv = pl.BlockSpec((None, seq, GQA_HEAD_DIM), lambda b, g, t: (b, 0, g))
    return pl.pallas_call(
        kern,
        grid=(bsz, GQA_KV_HEADS, seq // tq - row_off),
        in_specs=[pl.BlockSpec((None, tq, gwid), lambda b, g, t: (b, t + row_off, g)), kv, kv],
        out_specs=pl.BlockSpec((None, tq, gwid), lambda b, g, t: (b, t, g)),
        out_shape=jax.ShapeDtypeStruct((bsz, seq - row_off * tq, GQA_WIDTH), BF16),
        compiler_params=_params("parallel", "parallel", "arbitrary"),
    )(q, k, v)


def _diff_kernel(q_ref, k_ref, v_ref, lam_ref, nw_ref, o_ref, *, n_ctx_tiles, ctx_len, lam_init):
    lv = lam_ref[...]
    lam = (jnp.exp(jnp.sum(lv[0:1] * lv[1:2], axis=-1, keepdims=True))
           - jnp.exp(jnp.sum(lv[2:3] * lv[3:4], axis=-1, keepdims=True)) + lam_init)

    def attend(nk):
        k = k_ref[0:nk, :]
        v = v_ref[0:nk, :]
        q = q_ref[...]
        lane = lax.broadcasted_iota(jnp.int32, q.shape, 1)
        zero = jnp.zeros_like(q)
        tq = q.shape[0]
        q2 = jnp.concatenate([jnp.where(lane < DIFF_HEAD_DIM, q, zero),
                              jnp.where(lane >= DIFF_HEAD_DIM, q, zero)], axis=0)
        s = _dot_nt(q2, k)
        m = jnp.max(s, axis=-1, keepdims=True)
        p = jnp.exp(s - m)
        l = jnp.sum(p, axis=-1, keepdims=True)
        w = p[0:tq] * (1.0 / l[0:tq]) - p[tq:2 * tq] * (lam / l[tq:2 * tq])
        o = _dot(w.astype(BF16), v)
        ms = jnp.mean(o * o, axis=-1, keepdims=True)
        o_ref[...] = (o * lax.rsqrt(ms + EPS) * nw_ref[...] * (1.0 - lam_init)).astype(BF16)

    t = pl.program_id(2)

    @pl.when(t < n_ctx_tiles)
    def _():
        attend(ctx_len)

    @pl.when(t >= n_ctx_tiles)
    def _():
        attend(k_ref.shape[0])


def _diff_call(q, k, v, lam_p, nw, lam_init, ctx_len, tq, row_off):
    bsz, seq, _ = q.shape
    kern = functools.partial(_diff_kernel, n_ctx_tiles=ctx_len // tq - row_off, ctx_len=ctx_len,
                             lam_init=lam_init)
    kv = pl.BlockSpec((None, seq, LANES), lambda b, h, t: (b, 0, h))
    return pl.pallas_call(
        kern,
        grid=(bsz, DIFF_HEADS, seq // tq - row_off),
        in_specs=[pl.BlockSpec((None, tq, LANES), lambda b, h, t: (b, t + row_off, h)), kv, kv,
                  _const_spec(lam_p.shape), _const_spec(nw.shape)],
        out_specs=pl.BlockSpec((None, tq, LANES), lambda b, h, t: (b, t, h)),
        out_shape=jax.ShapeDtypeStruct((bsz, seq - row_off * tq, DIFF_WIDTH), BF16),
        compiler_params=_params("parallel", "parallel", "arbitrary"),
    )(q, k, v, lam_p, nw)


def _merge_kernel(x_ref, mod_ref, ys_ref, yg_ref, yd_ref, wg_ref, bg_ref, ws_ref, wq_ref, wd_ref,
                  wo_ref, lg_ref, lb_ref, o_ref):
    d = D_MODEL
    x = x_ref[...]
    h = _modulated(x_ref, mod_ref, 0)
    m = None
    for i, (y_ref, w_ref) in enumerate(((ys_ref, ws_ref), (yg_ref, wq_ref), (yd_ref, wd_ref))):
        gate = _sigmoid(_dot(h, wg_ref[:, i * d:(i + 1) * d]) + bg_ref[:, i * d:(i + 1) * d])
        term = gate * _dot(y_ref[...], w_ref[...])
        m = term if m is None else m + term
    y = _dot(m.astype(BF16), wo_ref[...])
    r = DEEPNORM_ALPHA * x + mod_ref[2:3, :] * y
    o_ref[...] = _layer_norm(r, lg_ref[...], lb_ref[...])


def _merge_call(xs, mods, ys, yg, yd, lw, ctx_len, tm, row_off):
    bsz, seq, d = xs.shape
    nct = ctx_len // tm - row_off
    full = lambda w: pl.BlockSpec((None, tm, w), lambda b, t: (b, t + row_off, 0))
    tok = lambda w: pl.BlockSpec((None, tm, w), lambda b, t: (b, t, 0))
    consts = [lw["w_gate"], lw["b_gate"], lw["w_ssm_out"], lw["w_gqa_out"], lw["w_diff_out"],
              lw["w_o"], lw["ln1_g"], lw["ln1_b"]]
    return pl.pallas_call(
        _merge_kernel,
        grid=(bsz, seq // tm - row_off),
        in_specs=[full(d), pl.BlockSpec((None, 6, d), _mod_index(nct)),
                  full(SSM_D_INNER), tok(GQA_WIDTH), tok(DIFF_WIDTH)]
                 + [_const_spec(a.shape) for a in consts],
        out_specs=tok(d),
        out_shape=jax.ShapeDtypeStruct((bsz, seq - row_off * tm, d), F32),
        compiler_params=_params("parallel", "parallel"),
    )(xs, mods, ys, yg, yd, *consts)


def _ffn_kernel(x_ref, mod_ref, wi_ref, wo_ref, lg_ref, lb_ref, o_ref):
    x = x_ref[...]
    h = _modulated(x_ref, mod_ref, 3)
    acc = None
    for c0 in range(0, FFN_HIDDEN, FFN_CHUNK):
        a = _dot(h, wi_ref[:, c0:c0 + FFN_CHUNK])
        b = _dot(h, wi_ref[:, FFN_HIDDEN + c0:FFN_HIDDEN + c0 + FFN_CHUNK])
        u = (_silu(a) * b).astype(BF16)
        part = _dot(u, wo_ref[c0:c0 + FFN_CHUNK, :])
        acc = part if acc is None else acc + part
    r = DEEPNORM_ALPHA * x + mod_ref[5:6, :] * acc
    o_ref[...] = _layer_norm(r, lg_ref[...], lb_ref[...])


def _ffn_call(xs, mods, lw, n_ctx_tiles, tm):
    bsz, seq, d = xs.shape
    tok = pl.BlockSpec((None, tm, d), lambda b, t: (b, t, 0))
    consts = [lw["ffn_w_in"], lw["ffn_w_out"], lw["ln2_g"], lw["ln2_b"]]
    return pl.pallas_call(
        _ffn_kernel,
        grid=(bsz, seq // tm),
        in_specs=[tok, pl.BlockSpec((None, 6, d), _mod_index(n_ctx_tiles))]
                 + [_const_spec(a.shape) for a in consts],
        out_specs=tok,
        out_shape=jax.ShapeDtypeStruct((bsz, seq, d), F32),
        compiler_params=_params("parallel", "parallel"),
    )(xs, mods, *consts)


def _rope_tables(ctx_len, lat_len, head_dim, half_sign_period):
    t = jnp.arange(lat_len, dtype=jnp.int32)
    d_axis = head_dim // 2
    inv_freq = ROPE_THETA ** (-jnp.arange(0, d_axis, 2, dtype=F32) / d_axis)
    ang_r = (t // GRID_W).astype(F32)[:, None] * inv_freq
    ang_c = (t % GRID_W).astype(F32)[:, None] * inv_freq
    ang = jnp.concatenate([ang_r, ang_r, ang_c, ang_c], axis=-1)
    cos, sin = jnp.cos(ang), jnp.sin(ang)
    reps = LANES // head_dim
    cos, sin = jnp.tile(cos, (1, reps)), jnp.tile(sin, (1, reps))
    lane = jnp.arange(LANES)
    sign = jnp.where((lane % half_sign_period) < half_sign_period // 2, -1.0, 1.0).astype(F32)
    cos = jnp.concatenate([jnp.ones((ctx_len, LANES), F32), cos], axis=0)
    sin = jnp.concatenate([jnp.zeros((ctx_len, LANES), F32), sin * sign], axis=0)
    return cos, sin


def _group_rows(v, width):
    return v.reshape(SSM_GROUPS, 1, width)


def _ssm_params(conv_w, conv_b, dt_bias, a_log, d_skip, norm_w):
    gh, ng = SSM_GROUP_HEADS, SSM_GROUPS

    def taps(lo, width):
        w = conv_w[:, lo:lo + ng * width].reshape(SSM_CONV, ng, width)
        b = conv_b[lo:lo + ng * width].reshape(1, ng, width)
        packed = jnp.concatenate(
            [w, b, jnp.zeros((SUBLANES - SSM_CONV - 1, ng, width), F32)], axis=0)
        return jnp.transpose(packed, (1, 0, 2))

    def per_dir(v):
        r = jnp.transpose(v.reshape(2, ng, gh), (1, 0, 2)).reshape(ng, 1, 2 * gh)
        return jnp.pad(r, ((0, 0), (0, 0), (0, DT_WIDTH - 2 * gh)))

    return {
        "cwx": taps(0, SSM_GROUP_WIDTH),
        "cwb": taps(SSM_D_INNER, SSM_STATE),
        "cwc": taps(SSM_D_INNER + SSM_BC, SSM_STATE),
        "dtb": per_dir(dt_bias), "alog": per_dir(a_log),
        "dskip": _group_rows(jnp.repeat(d_skip, SSM_HEAD_DIM), SSM_GROUP_WIDTH),
        "nw": _group_rows(norm_w, SSM_GROUP_WIDTH),
    }


def kernel(x, c, ctx, c_ctx, ada_w, ada_b, w_in, b_gate, ssm_conv_w, ssm_conv_b, ssm_dt_bias,
           ssm_a_log, ssm_d, ssm_norm_w, w_ssm_out, gqa_q_norm, gqa_k_norm, w_gqa_out, diff_lambda,
           diff_norm_w, w_diff_out, w_o, ln1_g, ln1_b, ffn_w_in, ffn_w_out, ln2_g, ln2_b):
    bsz, lat_len, d = x.shape
    ctx_len = ctx.shape[1]
    depth = w_in.shape[0]
    assert d == D_MODEL and depth == DEPTH and w_in.shape[2] == IN_WIDTH
    assert lat_len % GRID_W == 0 and lat_len % SSM_CHUNK == 0 and ctx_len % SSM_CHUNK == 0
    tm = math.gcd(256, ctx_len)
    tq = tm

    rows = -(-(bsz + 1) // SUBLANES) * SUBLANES
    cvec = jnp.concatenate([c, c_ctx[None], jnp.zeros((rows - bsz - 1, d), F32)], axis=0)
    ada = _ada_call(cvec, ada_w, ada_b)

    rope_g = _rope_tables(ctx_len, lat_len, GQA_HEAD_DIM, GQA_HEAD_DIM // 2)
    rope_d = _rope_tables(ctx_len, lat_len, DIFF_HEAD_DIM, DIFF_HEAD_DIM // 2)
    tabs = (*rope_g, *rope_d)

    xs = jnp.concatenate([ctx, x], axis=1)
    pad = jnp.zeros((d, DT_WIDTH - 2 * SSM_HEADS), BF16)
    out = None
    for i in range(depth):
        need_ctx = i < depth - 1
        lam_init = 0.8 - 0.6 * math.exp(-0.3 * i)
        mod_l = ada[i, :bsz].reshape(bsz, 1, 6, d)
        mod_c = jnp.broadcast_to(ada[i, bsz].reshape(1, 1, 6, d), (bsz, 1, 6, d))
        mods = jnp.concatenate([mod_c, mod_l], axis=1).reshape(2 * bsz, 6, d)
        w = w_in[i].astype(BF16)
        w_ssm = jnp.concatenate([w[:, :OFF_GQ], pad], axis=1)
        w_attn = w[:, OFF_GQ:OFF_GATE]
        lw = {
            "w_gate": w[:, OFF_GATE:], "b_gate": b_gate[i][None],
            "w_ssm_out": w_ssm_out[i].astype(BF16), "w_gqa_out": w_gqa_out[i].astype(BF16),
            "w_diff_out": w_diff_out[i].astype(BF16), "w_o": w_o[i].astype(BF16),
            "ln1_g": ln1_g[i][None], "ln1_b": ln1_b[i][None],
            "ffn_w_in": ffn_w_in[i].astype(BF16), "ffn_w_out": ffn_w_out[i].astype(BF16),
            "ln2_g": ln2_g[i][None], "ln2_b": ln2_b[i][None],
        }
        sp = _ssm_params(ssm_conv_w[i], ssm_conv_b[i], ssm_dt_bias[i], ssm_a_log[i], ssm_d[i],
                         ssm_norm_w[i])

        z, xbc, dt = _inproj_ssm_call(xs, mods, w_ssm, ctx_len, tm)
        gq, gk, gv, dq, dk, dv = _inproj_attn_call(
            xs, mods, w_attn, gqa_q_norm[i][None], gqa_k_norm[i][None], tabs, ctx_len, tm)
        y_ssm = _ssd_call(xbc, dt, z, sp, ctx_len)
        off = 0 if need_ctx else ctx_len // tm
        y_gqa = _gqa_call(gq, gk, gv, ctx_len, tq, off)
        y_diff = _diff_call(dq, dk, dv, diff_lambda[i], diff_norm_w[i][None], lam_init, ctx_len, tq,
                            off)
        x1 = _merge_call(xs, mods, y_ssm, y_gqa, y_diff, lw, ctx_len, tm, off)
        xs = _ffn_call(x1, mods, lw, ctx_len // tm - off, tm)
    return xs
```

```python
import functools
import math

import jax
import jax.numpy as jnp
from jax import lax
from jax.experimental import pallas as pl
from jax.experimental.pallas import tpu as pltpu

F32 = jnp.float32
BF16 = jnp.bfloat16

D_MODEL = 1024
DEPTH = 4
GRID_W = 64
ROPE_THETA = 10000.0
EPS = 1e-6

SSM_D_INNER = 2048
SSM_HEAD_DIM = 64
SSM_HEADS = 32
SSM_GROUPS = 4
SSM_STATE = 128
SSM_CONV = 5
SSM_CHUNK = 128
SSM_BC = SSM_GROUPS * SSM_STATE
SSM_CONV_DIM = SSM_D_INNER + 2 * SSM_BC
SSM_GROUP_WIDTH = SSM_D_INNER // SSM_GROUPS
SSM_GROUP_HEADS = SSM_HEADS // SSM_GROUPS

GQA_HEAD_DIM = 128
GQA_HEADS = 8
GQA_KV_HEADS = 2
GQA_GROUP = GQA_HEADS // GQA_KV_HEADS
GQA_WIDTH = GQA_HEADS * GQA_HEAD_DIM
GQA_KV_WIDTH = GQA_KV_HEADS * GQA_HEAD_DIM

DIFF_HEAD_DIM = 64
DIFF_HEADS = 8
DIFF_WIDTH = DIFF_HEADS * 2 * DIFF_HEAD_DIM

N_BRANCHES = 3
FFN_HIDDEN = 2816
FFN_CHUNK = 1408

DEEPNORM_ALPHA = (2 * DEPTH) ** 0.25

LANES = 128
SUBLANES = 8
DT_WIDTH = LANES
SSM_CONV_TILES = SSM_CONV_DIM // LANES
LOG2E = 1.4426950408889634
VMEM_LIMIT = 56 * 1024 * 1024

OFF_Z = 0
OFF_XBC = OFF_Z + SSM_D_INNER
OFF_DT = OFF_XBC + SSM_CONV_DIM
OFF_GQ = OFF_DT + 2 * SSM_HEADS
OFF_GK = OFF_GQ + GQA_WIDTH
OFF_GV = OFF_GK + GQA_KV_WIDTH
OFF_DQ = OFF_GV + GQA_KV_WIDTH
OFF_DK = OFF_DQ + DIFF_WIDTH
OFF_DV = OFF_DK + DIFF_WIDTH
OFF_GATE = OFF_DV + DIFF_WIDTH
IN_WIDTH = OFF_GATE + N_BRANCHES * D_MODEL


def _dot(a, b):
    return jnp.dot(a, b, preferred_element_type=F32)


def _dot_nt(a, b):
    return lax.dot_general(a, b, (((1,), (1,)), ((), ())), preferred_element_type=F32)


def _split3(a):
    hi = a.astype(BF16)
    r = a - hi.astype(F32)
    mid = r.astype(BF16)
    lo = (r - mid.astype(F32)).astype(BF16)
    return hi, mid, lo


def _dot_exact_rhs(a, sel):
    hi, mid, lo = _split3(a)
    return _dot(hi, sel) + _dot(mid, sel) + _dot(lo, sel)


def _sigmoid(x):
    return 1.0 / (1.0 + jnp.exp(-x))


def _silu(x):
    return x * _sigmoid(x)


def _layer_norm(r, g, b):
    mu = jnp.mean(r, axis=-1, keepdims=True)
    d = r - mu
    var = jnp.mean(d * d, axis=-1, keepdims=True)
    return d * lax.rsqrt(var + EPS) * g + b


def _params(*sem):
    return pltpu.CompilerParams(dimension_semantics=sem, vmem_limit_bytes=VMEM_LIMIT)


def _const_spec(shape):
    nd = len(shape)
    return pl.BlockSpec(shape, lambda *_: (0,) * nd, pipeline_mode=pl.Buffered(1))


def _ada_kernel(c_ref, w_ref, b_ref, o_ref):
    sc = _silu(c_ref[...]).astype(BF16)
    o_ref[...] = _dot(sc, w_ref[...].astype(BF16)) + b_ref[...]


def _ada_call(cvec, ada_w, ada_b):
    rows = cvec.shape[0]
    depth, d, n = ada_w.shape
    tn = n // 4
    return pl.pallas_call(
        _ada_kernel,
        grid=(depth, n // tn),
        in_specs=[
            pl.BlockSpec((rows, d), lambda i, j: (0, 0)),
            pl.BlockSpec((None, d, tn), lambda i, j: (i, 0, j)),
            pl.BlockSpec((None, 1, tn), lambda i, j: (i, 0, j)),
        ],
        out_specs=pl.BlockSpec((None, rows, tn), lambda i, j: (i, 0, j)),
        out_shape=jax.ShapeDtypeStruct((depth, rows, n), F32),
        compiler_params=_params("arbitrary", "arbitrary"),
        name="ada",
    )(cvec, ada_w, ada_b.reshape(depth, 1, n))


def _modulated(x_ref, mod_ref, shift_row):
    x = x_ref[...]
    shift = mod_ref[shift_row:shift_row + 1, :]
    scale = mod_ref[shift_row + 1:shift_row + 2, :]
    return (x * (1.0 + scale) + shift).astype(BF16)


def _inproj_ssm_kernel(x_ref, mod_ref, w_ref, z_ref, xbc_ref, dt_ref):
    h = _modulated(x_ref, mod_ref, 0)
    step = 512
    for c0 in range(0, SSM_D_INNER, step):
        z_ref[:, c0:c0 + step] = _dot(h, w_ref[:, OFF_Z + c0:OFF_Z + c0 + step])
    for c0 in range(0, SSM_CONV_DIM, step):
        u = _dot(h, w_ref[:, OFF_XBC + c0:OFF_XBC + c0 + step])
        for i in range(step // LANES):
            xbc_ref[c0 // LANES + i] = u[:, i * LANES:(i + 1) * LANES]
    dt_ref[...] = _dot(h, w_ref[:, OFF_DT:OFF_DT + DT_WIDTH])


def _mod_index(nct):
    return lambda b, t: (2 * b + jnp.where(t >= nct, 1, 0), 0, 0)


def _inproj_ssm_call(xs, mods, w_ssm, ctx_len, tm):
    bsz, seq, d = xs.shape
    nct = ctx_len // tm
    tok = lambda w: pl.BlockSpec((None, tm, w), lambda b, t: (b, t, 0))
    return pl.pallas_call(
        _inproj_ssm_kernel,
        grid=(bsz, seq // tm),
        in_specs=[tok(d), pl.BlockSpec((None, 6, d), _mod_index(nct)), _const_spec(w_ssm.shape)],
        out_specs=[tok(SSM_D_INNER),
                   pl.BlockSpec((None, SSM_CONV_TILES, tm, LANES), lambda b, t: (b, 0, t, 0)),
                   tok(DT_WIDTH)],
        out_shape=[jax.ShapeDtypeStruct((bsz, seq, SSM_D_INNER), F32),
                   jax.ShapeDtypeStruct((bsz, SSM_CONV_TILES, seq, LANES), F32),
                   jax.ShapeDtypeStruct((bsz, seq, DT_WIDTH), F32)],
        compiler_params=_params("parallel", "parallel"),
        name="inproj_ssm",
    )(xs, mods, w_ssm)


def _rope(u, cos, sin_signed, half):
    lane = lax.broadcasted_iota(jnp.int32, u.shape, 1)
    fwd = pltpu.roll(u, LANES - half, axis=1)
    bwd = pltpu.roll(u, half, axis=1)
    partner = jnp.where((lane % (2 * half)) < half, fwd, bwd)
    return u * cos + partner * sin_signed


def _rms_heads(u, g):
    ms = jnp.mean(u * u, axis=-1, keepdims=True)
    return u * lax.rsqrt(ms + EPS) * g


def _inproj_attn_kernel(x_ref, mod_ref, w_ref, qn_ref, kn_ref, cg_ref, sg_ref, cd_ref, sd_ref,
                        gq_ref, gk_ref, gv_ref, dq_ref, dk_ref, dv_ref):
    h = _modulated(x_ref, mod_ref, 0)
    base = OFF_GQ
    cg, sg = cg_ref[...], sg_ref[...]
    cd, sd = cd_ref[...], sd_ref[...]
    hd = GQA_HEAD_DIM
    for i in range(GQA_HEADS):
        u = _dot(h, w_ref[:, OFF_GQ - base + i * hd:OFF_GQ - base + (i + 1) * hd])
        u = _rope(_rms_heads(u, qn_ref[...]), cg, sg, GQA_HEAD_DIM // 4)
        gq_ref[:, i * hd:(i + 1) * hd] = u.astype(BF16)
    for i in range(GQA_KV_HEADS):
        u = _dot(h, w_ref[:, OFF_GK - base + i * hd:OFF_GK - base + (i + 1) * hd])
        u = _rope(_rms_heads(u, kn_ref[...]), cg, sg, GQA_HEAD_DIM // 4)
        gk_ref[:, i * hd:(i + 1) * hd] = u.astype(BF16)
    for i in range(GQA_KV_HEADS):
        u = _dot(h, w_ref[:, OFF_GV - base + i * hd:OFF_GV - base + (i + 1) * hd])
        gv_ref[i * hd:(i + 1) * hd, :] = u.T.astype(BF16)
    scale = DIFF_HEAD_DIM ** -0.5
    for i in range(DIFF_HEADS):
        u = _dot(h, w_ref[:, OFF_DQ - base + i * LANES:OFF_DQ - base + (i + 1) * LANES])
        dq_ref[:, i * LANES:(i + 1) * LANES] = (
            _rope(u, cd, sd, DIFF_HEAD_DIM // 4) * scale).astype(BF16)
    for i in range(DIFF_HEADS):
        u = _dot(h, w_ref[:, OFF_DK - base + i * LANES:OFF_DK - base + (i + 1) * LANES])
        dk_ref[:, i * LANES:(i + 1) * LANES] = _rope(u, cd, sd, DIFF_HEAD_DIM // 4).astype(BF16)
    for c0 in range(0, DIFF_WIDTH, 512):
        dv_ref[:, c0:c0 + 512] = _dot(
            h, w_ref[:, OFF_DV - base + c0:OFF_DV - base + c0 + 512]).astype(BF16)


def _inproj_attn_call(xs, mods, w_attn, qn, kn, tabs, ctx_len, tm):
    bsz, seq, d = xs.shape
    nct = ctx_len // tm
    tok = lambda w: pl.BlockSpec((None, tm, w), lambda b, t: (b, t, 0))
    tab = pl.BlockSpec((tm, LANES), lambda b, t: (t, 0))
    tr = lambda w: pl.BlockSpec((None, w, tm), lambda b, t: (b, 0, t))
    row = lambda w: jax.ShapeDtypeStruct((bsz, seq, w), BF16)
    col = lambda w: jax.ShapeDtypeStruct((bsz, w, seq), BF16)
    return pl.pallas_call(
        _inproj_attn_kernel,
        grid=(bsz, seq // tm),
        in_specs=[tok(d), pl.BlockSpec((None, 6, d), _mod_index(nct)), _const_spec(w_attn.shape),
                  _const_spec(qn.shape), _const_spec(kn.shape), tab, tab, tab, tab],
        out_specs=[tok(GQA_WIDTH), tok(GQA_KV_WIDTH), tr(GQA_KV_WIDTH),
                   tok(DIFF_WIDTH), tok(DIFF_WIDTH), tok(DIFF_WIDTH)],
        out_shape=[row(GQA_WIDTH), row(GQA_KV_WIDTH), col(GQA_KV_WIDTH),
                   row(DIFF_WIDTH), row(DIFF_WIDTH), row(DIFF_WIDTH)],
        compiler_params=_params("parallel", "parallel"),
        name="inproj_attn",
    )(xs, mods, w_attn, qn, kn, *tabs)


def _ssd_kernel(xr_ref, br_ref, cr_ref, dtr_ref, z_ref, cwx_ref, cwb_ref, cwc_ref,
                dtb_ref, alog_ref, dskip_ref, nw_ref, o_ref,
                xs_s, b_s, bt_s, c_s, acs_s, acst_s, dtt_s, yf_s, st_s, *, n_chunks, n_ctx_chunks):
    q = SSM_CHUNK
    gh = SSM_GROUP_HEADS
    hp = SSM_HEAD_DIM
    seq = n_chunks * q
    g = pl.program_id(1)

    row_i = lax.broadcasted_iota(jnp.int32, (q, q), 0)
    col_i = lax.broadcasted_iota(jnp.int32, (q, q), 1)
    tri_lo = (col_i <= row_i).astype(BF16)
    tri_up = (col_i >= row_i).astype(BF16)
    src = jnp.where(col_i < gh, g * gh + col_i, SSM_HEADS + g * gh + col_i - gh)
    pick = ((row_i == src) & (col_i < 2 * gh)).astype(BF16)
    a_neg = -jnp.exp(alog_ref[...])
    dt_bias = dtb_ref[...]

    def conv_silu(load, cw, c):
        r0 = c * q
        acc = None
        for k in range(SSM_CONV):
            off = k - SSM_CONV // 2
            start = r0 + off
            valid = None
            if start < 0:
                t = pltpu.roll(load(0), -off, axis=0)
                valid = row_i >= -off
            elif start + q > seq:
                t = pltpu.roll(load(seq - q), (q - off) % q, axis=0)
                valid = row_i < q - off
            else:
                t = load(start)
            if c == n_ctx_chunks and off < 0:
                valid = row_i >= -off
            if c == n_ctx_chunks - 1 and off > 0:
                valid = row_i < q - off
            if valid is not None:
                t = jnp.where(valid, t, 0.0)
            term = t * cw[k:k + 1, :]
            acc = term if acc is None else acc + term
        return _silu(acc + cw[SSM_CONV:SSM_CONV + 1, :])

    for c in range(n_chunks):
        r0 = c * q
        for j in range(SSM_GROUP_WIDTH // LANES):
            cols = slice(j * LANES, (j + 1) * LANES)
            xs_s[r0:r0 + q, cols] = conv_silu(
                lambda s, j=j: xr_ref[j, s:s + q, :], cwx_ref[:, cols], c)
        bc = conv_silu(lambda s: br_ref[s:s + q, :], cwb_ref[...], c)
        b_s[r0:r0 + q, :] = bc.astype(BF16)
        bt_s[c] = bc.T
        c_s[r0:r0 + q, :] = conv_silu(lambda s: cr_ref[s:s + q, :], cwc_ref[...], c)
        raw = _dot_exact_rhs(dtr_ref[r0:r0 + q, :], pick) + dt_bias
        dtv = jnp.maximum(raw, 0.0) + jnp.log1p(jnp.exp(-jnp.abs(raw)))
        parts = _split3(dtv * a_neg)
        hi, mid, lo = parts
        acs = jnp.where(col_i < gh, _dot(tri_lo, hi) + _dot(tri_lo, mid) + _dot(tri_lo, lo),
                        _dot(tri_up, hi) + _dot(tri_up, mid) + _dot(tri_up, lo))
        acs_s[r0:r0 + q, :] = acs
        acst_s[c] = acs.T
        dtt_s[c] = dtv.T

    low_half = col_i < hp

    def scan_chunk(c, direction):
        r0 = pl.multiple_of(c * q, q)
        rows = pl.ds(r0, q)
        xs = xs_s[rows, :]
        xb = xs.astype(BF16)
        cc = c_s[rows, :]
        acs = acs_s[rows, :]
        acst = acst_s[c]
        dtt = dtt_s[c]
        btf = bt_s[c]
        cb = _dot_nt(cc.astype(BF16), b_s[rows, :])
        end = q - 1 if direction == 0 else 0
        tot = acst[:, end:end + 1]
        mine = (row_i >= gh * direction) & (row_i < gh * (direction + 1))
        w_out = dtt * jnp.exp(jnp.where(mine, tot - acst, 0.0))
        e_tot = jnp.exp(tot)
        keep = (row_i >= col_i) if direction == 0 else (col_i >= row_i)
        outs = []
        for pair in range(gh // 2):
            ms, ces, bws, ets = [], [], [], []
            for hh in range(2):
                col = gh * direction + 2 * pair + hh
                a_l = jnp.broadcast_to(acs[:, col:col + 1], (q, q))
                decay = jnp.where(keep, jnp.exp(a_l - acst[col:col + 1, :]), 0.0)
                ms.append((cb * decay * dtt[col:col + 1, :]).astype(BF16))
                ces.append((cc * jnp.exp(a_l)).astype(BF16))
                bws.append((btf * w_out[col:col + 1, :]).astype(BF16))
                ets.append(e_tot[col:col + 1, :])
            slab = slice(pair * LANES, (pair + 1) * LANES)
            xp = xb[:, slab]
            zero = jnp.zeros_like(xp)
            x_bd = jnp.concatenate([jnp.where(low_half, xp, zero),
                                    jnp.where(low_half, zero, xp)], axis=0)
            st = st_s[:, slab]
            sb = st.astype(BF16)
            s_bd = jnp.concatenate([jnp.where(low_half, sb, zero),
                                    jnp.where(low_half, zero, sb)], axis=0)
            outs.append(_dot(jnp.concatenate(ms + ces, axis=1),
                             jnp.concatenate([x_bd, s_bd], axis=0)))
            grow = _dot(jnp.concatenate(bws, axis=1), x_bd)
            st_s[:, slab] = st * jnp.where(low_half, ets[0], ets[1]) + grow
        return rows, xs, jnp.concatenate(outs, axis=1)

    st_s[...] = jnp.zeros_like(st_s)

    def fwd_body(c, carry):
        rows, _, y = scan_chunk(c, 0)
        yf_s[rows, :] = y
        return carry

    lax.fori_loop(0, n_chunks, fwd_body, 0)

    st_s[...] = jnp.zeros_like(st_s)

    def bwd_body(i, carry):
        c = jnp.where(i < n_ctx_chunks, n_ctx_chunks - 1 - i, n_chunks - 1 - (i - n_ctx_chunks))
        rows, xs, yb = scan_chunk(c, 1)
        y = yf_s[rows, :] + yb + dskip_ref[...] * xs
        y = y * _silu(z_ref[rows, :])
        ms = jnp.mean(y * y, axis=-1, keepdims=True)
        o_ref[rows, :] = (y * lax.rsqrt(ms + EPS) * nw_ref[...]).astype(BF16)
        return carry

    lax.fori_loop(0, n_chunks, bwd_body, 0)


def _ssd_call(xbc, dt, z, sp, ctx_len):
    bsz, _, seq, _ = xbc.shape
    q, gw, ng = SSM_CHUNK, SSM_GROUP_WIDTH, SSM_GROUPS
    n_chunks = seq // q
    xt = gw // LANES
    bt0 = SSM_D_INNER // LANES
    ct0 = (SSM_D_INNER + SSM_BC) // LANES
    grp = lambda r, w: pl.BlockSpec((None, r, w), lambda b, g: (g, 0, 0))
    kern = functools.partial(_ssd_kernel, n_chunks=n_chunks, n_ctx_chunks=ctx_len // q)
    return pl.pallas_call(
        kern,
        grid=(bsz, ng),
        in_specs=[
            pl.BlockSpec((None, xt, seq, LANES), lambda b, g: (b, g, 0, 0)),
            pl.BlockSpec((None, None, seq, LANES), lambda b, g: (b, bt0 + g, 0, 0)),
            pl.BlockSpec((None, None, seq, LANES), lambda b, g: (b, ct0 + g, 0, 0)),
            pl.BlockSpec((None, seq, DT_WIDTH), lambda b, g: (b, 0, 0)),
            pl.BlockSpec((None, seq, gw), lambda b, g: (b, 0, g)),
            grp(SUBLANES, gw), grp(SUBLANES, SSM_STATE), grp(SUBLANES, SSM_STATE),
            grp(1, DT_WIDTH), grp(1, DT_WIDTH), grp(1, gw), grp(1, gw),
        ],
        out_specs=pl.BlockSpec((None, seq, gw), lambda b, g: (b, 0, g)),
        out_shape=jax.ShapeDtypeStruct((bsz, seq, SSM_D_INNER), BF16),
        scratch_shapes=[
            pltpu.VMEM((seq, gw), F32),
            pltpu.VMEM((seq, SSM_STATE), BF16),
            pltpu.VMEM((n_chunks, SSM_STATE, q), F32),
            pltpu.VMEM((seq, SSM_STATE), F32),
            pltpu.VMEM((seq, DT_WIDTH), F32),
            pltpu.VMEM((n_chunks, DT_WIDTH, q), F32),
            pltpu.VMEM((n_chunks, DT_WIDTH, q), F32),
            pltpu.VMEM((seq, gw), F32),
            pltpu.VMEM((SSM_STATE, gw), F32),
        ],
        compiler_params=_params("parallel", "arbitrary"),
        name="ssd",
    )(xbc, xbc, xbc, dt, z, sp["cwx"], sp["cwb"], sp["cwc"], sp["dtb"], sp["alog"],
      sp["dskip"], sp["nw"])


def _interleave(*stages):
    live = list(stages)
    while live:
        for st in list(live):
            if next(st, StopIteration) is StopIteration:
                live.remove(st)


def _gqa_kernel(q_ref, k_ref, vt_ref, o_ref, s_scr, *, ctx_len, tq, skip_ctx):
    c2 = GQA_HEAD_DIM ** -0.5 * LOG2E
    seq = k_ref.shape[0]
    kc = tq

    def attend(q_start, nk, o_start):
        chunks = [slice(j * kc, (j + 1) * kc) for j in range(nk // kc)]
        q_rows = pl.ds(q_start, tq)
        o_rows = pl.ds(o_start, tq)
        row_max = {}

        def scores(r):
            q = q_ref[q_rows, r * GQA_HEAD_DIM:(r + 1) * GQA_HEAD_DIM]
            m = None
            for ch in chunks:
                s = _dot_nt(k_ref[ch, :], q)
                s_scr[r % 2, ch, :] = s
                mj = jnp.max(s, axis=0, keepdims=True)
                m = mj if m is None else jnp.maximum(m, mj)
                yield
            row_max[r] = m

        def values(r):
            m = row_max[r]
            l = None
            acc = None
            for ch in chunks:
                p = jnp.exp2((s_scr[r % 2, ch, :] - m) * c2)
                lj = jnp.sum(p, axis=0, keepdims=True)
                l = lj if l is None else l + lj
                t = _dot(vt_ref[:, ch], p.astype(BF16))
                acc = t if acc is None else acc + t
                yield
            o_ref[o_rows, r * GQA_HEAD_DIM:(r + 1) * GQA_HEAD_DIM] = (acc / l).T.astype(BF16)

        _interleave(scores(0))
        for r in range(GQA_GROUP):
            if r + 1 < GQA_GROUP:
                _interleave(values(r), scores(r + 1))
            else:
                _interleave(values(r))

    _attend_all(attend, ctx_len, seq, tq, skip_ctx)


def _attend_all(attend, ctx_len, seq, tq, skip_ctx):
    o_shift = ctx_len if skip_ctx else 0
    if not skip_ctx:
        for i in range(ctx_len // tq):
            attend(i * tq, ctx_len, i * tq)

    def body(i, carry):
        start = pl.multiple_of(ctx_len + i * tq, tq)
        attend(start, seq, pl.multiple_of(start - o_shift, tq))
        return carry

    lax.fori_loop(0, (seq - ctx_len) // tq, body, 0)


def _gqa_call(q, k, vt, ctx_len, tq, skip_ctx):
    bsz, seq, _ = q.shape
    gwid = GQA_GROUP * GQA_HEAD_DIM
    out_rows = seq - ctx_len if skip_ctx else seq
    kern = functools.partial(_gqa_kernel, ctx_len=ctx_len, tq=tq, skip_ctx=skip_ctx)
    return pl.pallas_call(
        kern,
        grid=(bsz, GQA_KV_HEADS),
        in_specs=[pl.BlockSpec((None, seq, gwid), lambda b, g: (b, 0, g)),
                  pl.BlockSpec((None, seq, GQA_HEAD_DIM), lambda b, g: (b, 0, g)),
                  pl.BlockSpec((None, GQA_HEAD_DIM, seq), lambda b, g: (b, g, 0))],
        out_specs=pl.BlockSpec((None, out_rows, gwid), lambda b, g: (b, 0, g)),
        out_shape=jax.ShapeDtypeStruct((bsz, out_rows, GQA_WIDTH), BF16),
        scratch_shapes=[pltpu.VMEM((2, seq, tq), F32)],
        compiler_params=_params("parallel", "parallel"),
        name="gqa",
    )(q, k, vt)


def _diff_kernel(q_ref, k_ref, v_ref, lam_ref, nw_ref, o_ref, *, ctx_len, tq, skip_ctx, lam_init):
    lv = lam_ref[...]
    lam = (jnp.exp(jnp.sum(lv[0:1] * lv[1:2], axis=-1, keepdims=True))
           - jnp.exp(jnp.sum(lv[2:3] * lv[3:4], axis=-1, keepdims=True)) + lam_init)
    seq = k_ref.shape[0]

    def attend(q_start, nk, o_start):
        q = q_ref[pl.ds(q_start, tq), :]
        lane = lax.broadcasted_iota(jnp.int32, q.shape, 1)
        zero = jnp.zeros_like(q)
        q2 = jnp.concatenate([jnp.where(lane < DIFF_HEAD_DIM, q, zero),
                              jnp.where(lane >= DIFF_HEAD_DIM, q, zero)], axis=0)
        s = _dot_nt(q2, k_ref[0:nk, :])
        m = jnp.max(s, axis=-1, keepdims=True)
        p = jnp.exp(s - m)
        l = jnp.sum(p, axis=-1, keepdims=True)
        w = p[0:tq] * (1.0 / l[0:tq]) - p[tq:2 * tq] * (lam / l[tq:2 * tq])
        o = _dot(w.astype(BF16), v_ref[0:nk, :])
        ms = jnp.mean(o * o, axis=-1, keepdims=True)
        o_ref[pl.ds(o_start, tq), :] = (
            o * lax.rsqrt(ms + EPS) * nw_ref[...] * (1.0 - lam_init)).astype(BF16)

    _attend_all(attend, ctx_len, seq, tq, skip_ctx)


def _diff_call(q, k, v, lam_p, nw, lam_init, ctx_len, tq, skip_ctx):
    bsz, seq, _ = q.shape
    out_rows = seq - ctx_len if skip_ctx else seq
    kern = functools.partial(_diff_kernel, ctx_len=ctx_len, tq=tq, skip_ctx=skip_ctx,
                             lam_init=lam_init)
    head = pl.BlockSpec((None, seq, LANES), lambda b, h: (b, 0, h))
    return pl.pallas_call(
        kern,
        grid=(bsz, DIFF_HEADS),
        in_specs=[head, head, head, _const_spec(lam_p.shape), _const_spec(nw.shape)],
        out_specs=pl.BlockSpec((None, out_rows, LANES), lambda b, h: (b, 0, h)),
        out_shape=jax.ShapeDtypeStruct((bsz, out_rows, DIFF_WIDTH), BF16),
        compiler_params=_params("parallel", "parallel"),
        name="diff_attn",
    )(q, k, v, lam_p, nw)


def _merge_kernel(x_ref, mod_ref, ys_ref, yg_ref, yd_ref, wg_ref, bg_ref, ws_ref, wq_ref, wd_ref,
                  wo_ref, lg_ref, lb_ref, o_ref):
    d = D_MODEL
    x = x_ref[...]
    h = _modulated(x_ref, mod_ref, 0)
    m = None
    for i, (y_ref, w_ref) in enumerate(((ys_ref, ws_ref), (yg_ref, wq_ref), (yd_ref, wd_ref))):
        gate = _sigmoid(_dot(h, wg_ref[:, i * d:(i + 1) * d]) + bg_ref[:, i * d:(i + 1) * d])
        term = gate * _dot(y_ref[...], w_ref[...])
        m = term if m is None else m + term
    y = _dot(m.astype(BF16), wo_ref[...])
    r = DEEPNORM_ALPHA * x + mod_ref[2:3, :] * y
    o_ref[...] = _layer_norm(r, lg_ref[...], lb_ref[...])


def _merge_call(xs, mods, ys, yg, yd, lw, ctx_len, tm, row_off):
    bsz, seq, d = xs.shape
    nct = ctx_len // tm - row_off
    full = lambda w: pl.BlockSpec((None, tm, w), lambda b, t: (b, t + row_off, 0))
    tok = lambda w: pl.BlockSpec((None, tm, w), lambda b, t: (b, t, 0))
    consts = [lw["w_gate"], lw["b_gate"], lw["w_ssm_out"], lw["w_gqa_out"], lw["w_diff_out"],
              lw["w_o"], lw["ln1_g"], lw["ln1_b"]]
    return pl.pallas_call(
        _merge_kernel,
        grid=(bsz, seq // tm - row_off),
        in_specs=[full(d), pl.BlockSpec((None, 6, d), _mod_index(nct)),
                  full(SSM_D_INNER), tok(GQA_WIDTH), tok(DIFF_WIDTH)]
                 + [_const_spec(a.shape) for a in consts],
        out_specs=tok(d),
        out_shape=jax.ShapeDtypeStruct((bsz, seq - row_off * tm, d), F32),
        compiler_params=_params("parallel", "parallel"),
        name="merge",
    )(xs, mods, ys, yg, yd, *consts)


def _ffn_kernel(x_ref, mod_ref, wi_ref, wo_ref, lg_ref, lb_ref, o_ref):
    x = x_ref[...]
    h = _modulated(x_ref, mod_ref, 3)
    acc = None
    for c0 in range(0, FFN_HIDDEN, FFN_CHUNK):
        a = _dot(h, wi_ref[:, c0:c0 + FFN_CHUNK])
        b = _dot(h, wi_ref[:, FFN_HIDDEN + c0:FFN_HIDDEN + c0 + FFN_CHUNK])
        u = (_silu(a) * b).astype(BF16)
        part = _dot(u, wo_ref[c0:c0 + FFN_CHUNK, :])
        acc = part if acc is None else acc + part
    r = DEEPNORM_ALPHA * x + mod_ref[5:6, :] * acc
    o_ref[...] = _layer_norm(r, lg_ref[...], lb_ref[...])


def _ffn_call(xs, mods, lw, n_ctx_tiles, tm):
    bsz, seq, d = xs.shape
    tok = pl.BlockSpec((None, tm, d), lambda b, t: (b, t, 0))
    consts = [lw["ffn_w_in"], lw["ffn_w_out"], lw["ln2_g"], lw["ln2_b"]]
    return pl.pallas_call(
        _ffn_kernel,
        grid=(bsz, seq // tm),
        in_specs=[tok, pl.BlockSpec((None, 6, d), _mod_index(n_ctx_tiles))]
                 + [_const_spec(a.shape) for a in consts],
        out_specs=tok,
        out_shape=jax.ShapeDtypeStruct((bsz, seq, d), F32),
        compiler_params=_params("parallel", "parallel"),
        name="ffn",
    )(xs, mods, *consts)


def _rope_tables(ctx_len, lat_len, head_dim, half_sign_period):
    t = jnp.arange(lat_len, dtype=jnp.int32)
    d_axis = head_dim // 2
    inv_freq = ROPE_THETA ** (-jnp.arange(0, d_axis, 2, dtype=F32) / d_axis)
    ang_r = (t // GRID_W).astype(F32)[:, None] * inv_freq
    ang_c = (t % GRID_W).astype(F32)[:, None] * inv_freq
    ang = jnp.concatenate([ang_r, ang_r, ang_c, ang_c], axis=-1)
    cos, sin = jnp.cos(ang), jnp.sin(ang)
    reps = LANES // head_dim
    cos, sin = jnp.tile(cos, (1, reps)), jnp.tile(sin, (1, reps))
    lane = jnp.arange(LANES)
    sign = jnp.where((lane % half_sign_period) < half_sign_period // 2, -1.0, 1.0).astype(F32)
    cos = jnp.concatenate([jnp.ones((ctx_len, LANES), F32), cos], axis=0)
    sin = jnp.concatenate([jnp.zeros((ctx_len, LANES), F32), sin * sign], axis=0)
    return cos, sin


def _group_rows(v, width):
    return v.reshape(SSM_GROUPS, 1, width)


def _ssm_params(conv_w, conv_b, dt_bias, a_log, d_skip, norm_w):
    gh, ng = SSM_GROUP_HEADS, SSM_GROUPS

    def taps(lo, width):
        w = conv_w[:, lo:lo + ng * width].reshape(SSM_CONV, ng, width)
        b = conv_b[lo:lo + ng * width].reshape(1, ng, width)
        packed = jnp.concatenate(
            [w, b, jnp.zeros((SUBLANES - SSM_CONV - 1, ng, width), F32)], axis=0)
        return jnp.transpose(packed, (1, 0, 2))

    def per_dir(v):
        r = jnp.transpose(v.reshape(2, ng, gh), (1, 0, 2)).reshape(ng, 1, 2 * gh)
        return jnp.pad(r, ((0, 0), (0, 0), (0, DT_WIDTH - 2 * gh)))

    return {
        "cwx": taps(0, SSM_GROUP_WIDTH),
        "cwb": taps(SSM_D_INNER, SSM_STATE),
        "cwc": taps(SSM_D_INNER + SSM_BC, SSM_STATE),
        "dtb": per_dir(dt_bias), "alog": per_dir(a_log),
        "dskip": _group_rows(jnp.repeat(d_skip, SSM_HEAD_DIM), SSM_GROUP_WIDTH),
        "nw": _group_rows(norm_w, SSM_GROUP_WIDTH),
    }


def kernel(x, c, ctx, c_ctx, ada_w, ada_b, w_in, b_gate, ssm_conv_w, ssm_conv_b, ssm_dt_bias,
           ssm_a_log, ssm_d, ssm_norm_w, w_ssm_out, gqa_q_norm, gqa_k_norm, w_gqa_out, diff_lambda,
           diff_norm_w, w_diff_out, w_o, ln1_g, ln1_b, ffn_w_in, ffn_w_out, ln2_g, ln2_b):
    bsz, lat_len, d = x.shape
    ctx_len = ctx.shape[1]
    depth = w_in.shape[0]
    assert d == D_MODEL and depth == DEPTH and w_in.shape[2] == IN_WIDTH
    assert lat_len % GRID_W == 0 and lat_len % SSM_CHUNK == 0 and ctx_len % SSM_CHUNK == 0
    tm = math.gcd(256, ctx_len)
    tq = tm

    rows = -(-(bsz + 1) // SUBLANES) * SUBLANES
    cvec = jnp.concatenate([c, c_ctx[None], jnp.zeros((rows - bsz - 1, d), F32)], axis=0)
    ada = _ada_call(cvec, ada_w, ada_b)

    rope_g = _rope_tables(ctx_len, lat_len, GQA_HEAD_DIM, GQA_HEAD_DIM // 2)
    rope_d = _rope_tables(ctx_len, lat_len, DIFF_HEAD_DIM, DIFF_HEAD_DIM // 2)
    tabs = (*rope_g, *rope_d)

    xs = jnp.concatenate([ctx, x], axis=1)
    pad = jnp.zeros((d, DT_WIDTH - 2 * SSM_HEADS), BF16)
    for i in range(depth):
        need_ctx = i < depth - 1
        lam_init = 0.8 - 0.6 * math.exp(-0.3 * i)
        mod_l = ada[i, :bsz].reshape(bsz, 1, 6, d)
        mod_c = jnp.broadcast_to(ada[i, bsz].reshape(1, 1, 6, d), (bsz, 1, 6, d))
        mods = jnp.concatenate([mod_c, mod_l], axis=1).reshape(2 * bsz, 6, d)
        w = w_in[i].astype(BF16)
        w_ssm = jnp.concatenate([w[:, :OFF_GQ], pad], axis=1)
        w_attn = w[:, OFF_GQ:OFF_GATE]
        lw = {
            "w_gate": w[:, OFF_GATE:], "b_gate": b_gate[i][None],
            "w_ssm_out": w_ssm_out[i].astype(BF16), "w_gqa_out": w_gqa_out[i].astype(BF16),
            "w_diff_out": w_diff_out[i].astype(BF16), "w_o": w_o[i].astype(BF16),
            "ln1_g": ln1_g[i][None], "ln1_b": ln1_b[i][None],
            "ffn_w_in": ffn_w_in[i].astype(BF16), "ffn_w_out": ffn_w_out[i].astype(BF16),
            "ln2_g": ln2_g[i][None], "ln2_b": ln2_b[i][None],
        }
        sp = _ssm_params(ssm_conv_w[i], ssm_conv_b[i], ssm_dt_bias[i], ssm_a_log[i], ssm_d[i],
                         ssm_norm_w[i])

        z, xbc, dt = _inproj_ssm_call(xs, mods, w_ssm, ctx_len, tm)
        gq, gk, gvt, dq, dk, dv = _inproj_attn_call(
            xs, mods, w_attn, gqa_q_norm[i][None], gqa_k_norm[i][None], tabs, ctx_len, tm)
        y_ssm = _ssd_call(xbc, dt, z, sp, ctx_len)
        off = 0 if need_ctx else ctx_len // tm
        y_gqa = _gqa_call(gq, gk, gvt, ctx_len, tq, not need_ctx)
        y_diff = _diff_call(dq, dk, dv, diff_lambda[i], diff_norm_w[i][None], lam_init, ctx_len, tq,
                            not need_ctx)
        x1 = _merge_call(xs, mods, y_ssm, y_gqa, y_diff, lw, ctx_len, tm, off)
        xs = _ffn_call(x1, mods, lw, ctx_len // tm - off, tm)
    return xs
```

```python
import functools
import math

import jax
import jax.numpy as jnp
from jax import lax
from jax.experimental import pallas as pl
from jax.experimental.pallas import tpu as pltpu

F32 = jnp.float32
BF16 = jnp.bfloat16

D_MODEL = 1024
DEPTH = 4
GRID_W = 64
ROPE_THETA = 10000.0
EPS = 1e-6

SSM_D_INNER = 2048
SSM_HEAD_DIM = 64
SSM_HEADS = 32
SSM_GROUPS = 4
SSM_STATE = 128
SSM_CONV = 5
SSM_CHUNK = 128
SSM_BC = SSM_GROUPS * SSM_STATE
SSM_CONV_DIM = SSM_D_INNER + 2 * SSM_BC
SSM_GROUP_WIDTH = SSM_D_INNER // SSM_GROUPS
SSM_GROUP_HEADS = SSM_HEADS // SSM_GROUPS

GQA_HEAD_DIM = 128
GQA_HEADS = 8
GQA_KV_HEADS = 2
GQA_GROUP = GQA_HEADS // GQA_KV_HEADS
GQA_WIDTH = GQA_HEADS * GQA_HEAD_DIM
GQA_KV_WIDTH = GQA_KV_HEADS * GQA_HEAD_DIM

DIFF_HEAD_DIM = 64
DIFF_HEADS = 8
DIFF_WIDTH = DIFF_HEADS * 2 * DIFF_HEAD_DIM

N_BRANCHES = 3
FFN_HIDDEN = 2816
FFN_CHUNK = 1408

DEEPNORM_ALPHA = (2 * DEPTH) ** 0.25

LANES = 128
SUBLANES = 8
DT_WIDTH = LANES
SSM_CONV_TILES = SSM_CONV_DIM // LANES
LOG2E = 1.4426950408889634
VMEM_LIMIT = 56 * 1024 * 1024

OFF_Z = 0
OFF_XBC = OFF_Z + SSM_D_INNER
OFF_DT = OFF_XBC + SSM_CONV_DIM
OFF_GQ = OFF_DT + 2 * SSM_HEADS
OFF_GK = OFF_GQ + GQA_WIDTH
OFF_GV = OFF_GK + GQA_KV_WIDTH
OFF_DQ = OFF_GV + GQA_KV_WIDTH
OFF_DK = OFF_DQ + DIFF_WIDTH
OFF_DV = OFF_DK + DIFF_WIDTH
OFF_GATE = OFF_DV + DIFF_WIDTH
IN_WIDTH = OFF_GATE + N_BRANCHES * D_MODEL


def _dot(a, b):
    return jnp.dot(a, b, preferred_element_type=F32)


def _dot_nt(a, b):
    return lax.dot_general(a, b, (((1,), (1,)), ((), ())), preferred_element_type=F32)


def _split3(a):
    hi = a.astype(BF16)
    r = a - hi.astype(F32)
    mid = r.astype(BF16)
    lo = (r - mid.astype(F32)).astype(BF16)
    return hi, mid, lo


def _dot_exact_rhs(a, sel):
    hi, mid, lo = _split3(a)
    return _dot(hi, sel) + _dot(mid, sel) + _dot(lo, sel)


def _sigmoid(x):
    return 1.0 / (1.0 + jnp.exp(-x))


def _silu(x):
    return x * _sigmoid(x)


def _layer_norm(r, g, b):
    mu = jnp.mean(r, axis=-1, keepdims=True)
    d = r - mu
    var = jnp.mean(d * d, axis=-1, keepdims=True)
    return d * lax.rsqrt(var + EPS) * g + b


def _params(*sem):
    return pltpu.CompilerParams(dimension_semantics=sem, vmem_limit_bytes=VMEM_LIMIT)


def _const_spec(shape):
    nd = len(shape)
    return pl.BlockSpec(shape, lambda *_: (0,) * nd, pipeline_mode=pl.Buffered(1))


def _ada_kernel(c_ref, w_ref, b_ref, o_ref):
    sc = _silu(c_ref[...]).astype(BF16)
    o_ref[...] = _dot(sc, w_ref[...].astype(BF16)) + b_ref[...]


def _ada_call(cvec, ada_w, ada_b):
    rows = cvec.shape[0]
    depth, d, n = ada_w.shape
    tn = n // 4
    return pl.pallas_call(
        _ada_kernel,
        grid=(depth, n // tn),
        in_specs=[
            pl.BlockSpec((rows, d), lambda i, j: (0, 0)),
            pl.BlockSpec((None, d, tn), lambda i, j: (i, 0, j)),
            pl.BlockSpec((None, 1, tn), lambda i, j: (i, 0, j)),
        ],
        out_specs=pl.BlockSpec((None, rows, tn), lambda i, j: (i, 0, j)),
        out_shape=jax.ShapeDtypeStruct((depth, rows, n), F32),
        compiler_params=_params("arbitrary", "arbitrary"),
        name="ada",
    )(cvec, ada_w, ada_b.reshape(depth, 1, n))


def _modulated(x_ref, mod_ref, shift_row):
    x = x_ref[...]
    shift = mod_ref[shift_row:shift_row + 1, :]
    scale = mod_ref[shift_row + 1:shift_row + 2, :]
    return (x * (1.0 + scale) + shift).astype(BF16)


def _inproj_ssm_kernel(x_ref, mod_ref, w_ref, z_ref, xbc_ref, dt_ref):
    h = _modulated(x_ref, mod_ref, 0)
    step = 512
    for c0 in range(0, SSM_D_INNER, step):
        z_ref[:, c0:c0 + step] = _dot(h, w_ref[:, OFF_Z + c0:OFF_Z + c0 + step])
    for c0 in range(0, SSM_CONV_DIM, step):
        u = _dot(h, w_ref[:, OFF_XBC + c0:OFF_XBC + c0 + step])
        for i in range(step // LANES):
            xbc_ref[c0 // LANES + i] = u[:, i * LANES:(i + 1) * LANES]
    dt_ref[...] = _dot(h, w_ref[:, OFF_DT:OFF_DT + DT_WIDTH])


def _mod_index(nct):
    return lambda b, t: (2 * b + jnp.where(t >= nct, 1, 0), 0, 0)


def _inproj_ssm_call(xs, mods, w_ssm, ctx_len, tm):
    bsz, seq, d = xs.shape
    nct = ctx_len // tm
    tok = lambda w: pl.BlockSpec((None, tm, w), lambda b, t: (b, t, 0))
    return pl.pallas_call(
        _inproj_ssm_kernel,
        grid=(bsz, seq // tm),
        in_specs=[tok(d), pl.BlockSpec((None, 6, d), _mod_index(nct)), _const_spec(w_ssm.shape)],
        out_specs=[tok(SSM_D_INNER),
                   pl.BlockSpec((None, SSM_CONV_TILES, tm, LANES), lambda b, t: (b, 0, t, 0)),
                   tok(DT_WIDTH)],
        out_shape=[jax.ShapeDtypeStruct((bsz, seq, SSM_D_INNER), F32),
                   jax.ShapeDtypeStruct((bsz, SSM_CONV_TILES, seq, LANES), F32),
                   jax.ShapeDtypeStruct((bsz, seq, DT_WIDTH), F32)],
        compiler_params=_params("parallel", "parallel"),
        name="inproj_ssm",
    )(xs, mods, w_ssm)


def _rope(u, cos, sin_signed, half):
    lane = lax.broadcasted_iota(jnp.int32, u.shape, 1)
    fwd = pltpu.roll(u, LANES - half, axis=1)
    bwd = pltpu.roll(u, half, axis=1)
    partner = jnp.where((lane % (2 * half)) < half, fwd, bwd)
    return u * cos + partner * sin_signed


def _rms_heads(u, g):
    ms = jnp.mean(u * u, axis=-1, keepdims=True)
    return u * lax.rsqrt(ms + EPS) * g


def _inproj_attn_kernel(x_ref, mod_ref, w_ref, qn_ref, kn_ref, cg_ref, sg_ref, cd_ref, sd_ref,
                        gq_ref, gk_ref, gv_ref, dq_ref, dk_ref, dv_ref):
    h = _modulated(x_ref, mod_ref, 0)
    base = OFF_GQ
    cg, sg = cg_ref[...], sg_ref[...]
    cd, sd = cd_ref[...], sd_ref[...]
    hd = GQA_HEAD_DIM
    gqa_scale = GQA_HEAD_DIM ** -0.5 * LOG2E
    for i in range(GQA_HEADS):
        u = _dot(h, w_ref[:, OFF_GQ - base + i * hd:OFF_GQ - base + (i + 1) * hd])
        u = _rope(_rms_heads(u, qn_ref[...]), cg, sg, GQA_HEAD_DIM // 4)
        gq_ref[:, i * hd:(i + 1) * hd] = (u * gqa_scale).astype(BF16)
    for i in range(GQA_KV_HEADS):
        u = _dot(h, w_ref[:, OFF_GK - base + i * hd:OFF_GK - base + (i + 1) * hd])
        u = _rope(_rms_heads(u, kn_ref[...]), cg, sg, GQA_HEAD_DIM // 4)
        gk_ref[:, i * hd:(i + 1) * hd] = u.astype(BF16)
    for i in range(GQA_KV_HEADS):
        u = _dot(h, w_ref[:, OFF_GV - base + i * hd:OFF_GV - base + (i + 1) * hd])
        gv_ref[i * hd:(i + 1) * hd, :] = u.T.astype(BF16)
    scale = DIFF_HEAD_DIM ** -0.5 * LOG2E
    for i in range(DIFF_HEADS):
        u = _dot(h, w_ref[:, OFF_DQ - base + i * LANES:OFF_DQ - base + (i + 1) * LANES])
        dq_ref[:, i * LANES:(i + 1) * LANES] = (
            _rope(u, cd, sd, DIFF_HEAD_DIM // 4) * scale).astype(BF16)
    for i in range(DIFF_HEADS):
        u = _dot(h, w_ref[:, OFF_DK - base + i * LANES:OFF_DK - base + (i + 1) * LANES])
        dk_ref[:, i * LANES:(i + 1) * LANES] = _rope(u, cd, sd, DIFF_HEAD_DIM // 4).astype(BF16)
    for c0 in range(0, DIFF_WIDTH, 512):
        dv_ref[:, c0:c0 + 512] = _dot(
            h, w_ref[:, OFF_DV - base + c0:OFF_DV - base + c0 + 512]).astype(BF16)


def _inproj_attn_call(xs, mods, w_attn, qn, kn, tabs, ctx_len, tm):
    bsz, seq, d = xs.shape
    nct = ctx_len // tm
    tok = lambda w: pl.BlockSpec((None, tm, w), lambda b, t: (b, t, 0))
    tab = pl.BlockSpec((tm, LANES), lambda b, t: (t, 0))
    tr = lambda w: pl.BlockSpec((None, w, tm), lambda b, t: (b, 0, t))
    row = lambda w: jax.ShapeDtypeStruct((bsz, seq, w), BF16)
    col = lambda w: jax.ShapeDtypeStruct((bsz, w, seq), BF16)
    return pl.pallas_call(
        _inproj_attn_kernel,
        grid=(bsz, seq // tm),
        in_specs=[tok(d), pl.BlockSpec((None, 6, d), _mod_index(nct)), _const_spec(w_attn.shape),
                  _const_spec(qn.shape), _const_spec(kn.shape), tab, tab, tab, tab],
        out_specs=[tok(GQA_WIDTH), tok(GQA_KV_WIDTH), tr(GQA_KV_WIDTH),
                   tok(DIFF_WIDTH), tok(DIFF_WIDTH), tok(DIFF_WIDTH)],
        out_shape=[row(GQA_WIDTH), row(GQA_KV_WIDTH), col(GQA_KV_WIDTH),
                   row(DIFF_WIDTH), row(DIFF_WIDTH), row(DIFF_WIDTH)],
        compiler_params=_params("parallel", "parallel"),
        name="inproj_attn",
    )(xs, mods, w_attn, qn, kn, *tabs)


def _ssd_kernel(xr_ref, br_ref, cr_ref, dtr_ref, z_ref, cwx_ref, cwb_ref, cwc_ref,
                dtb_ref, alog_ref, dskip_ref, nw_ref, o_ref,
                xs_s, bt_s, c_s, cb_s, acs_s, acst_s, dtt_s, y_s, stf_s, stb_s, *,
                n_chunks, n_ctx_chunks):
    q = SSM_CHUNK
    gh = SSM_GROUP_HEADS
    hp = SSM_HEAD_DIM
    seq = n_chunks * q
    g = pl.program_id(1)

    row_i = lax.broadcasted_iota(jnp.int32, (q, q), 0)
    col_i = lax.broadcasted_iota(jnp.int32, (q, q), 1)
    tri_lo = (col_i <= row_i).astype(BF16)
    tri_up = (col_i >= row_i).astype(BF16)
    src = jnp.where(col_i < gh, g * gh + col_i, SSM_HEADS + g * gh + col_i - gh)
    pick = ((row_i == src) & (col_i < 2 * gh)).astype(BF16)
    a_neg = -jnp.exp(alog_ref[...])
    dt_bias = dtb_ref[...]

    def conv_silu(load, cw, c):
        r0 = c * q
        acc = None
        for k in range(SSM_CONV):
            off = k - SSM_CONV // 2
            start = r0 + off
            valid = None
            if start < 0:
                t = pltpu.roll(load(0), -off, axis=0)
                valid = row_i >= -off
            elif start + q > seq:
                t = pltpu.roll(load(seq - q), (q - off) % q, axis=0)
                valid = row_i < q - off
            else:
                t = load(start)
            if c == n_ctx_chunks and off < 0:
                valid = row_i >= -off
            if c == n_ctx_chunks - 1 and off > 0:
                valid = row_i < q - off
            if valid is not None:
                t = jnp.where(valid, t, 0.0)
            term = t * cw[k:k + 1, :]
            acc = term if acc is None else acc + term
        return _silu(acc + cw[SSM_CONV:SSM_CONV + 1, :])

    for c in range(n_chunks):
        r0 = c * q
        for j in range(SSM_GROUP_WIDTH // LANES):
            cols = slice(j * LANES, (j + 1) * LANES)
            xs_s[r0:r0 + q, cols] = conv_silu(
                lambda s, j=j: xr_ref[j, s:s + q, :], cwx_ref[:, cols], c)
        bc = conv_silu(lambda s: br_ref[s:s + q, :], cwb_ref[...], c)
        bt_s[c] = bc.T
        cconv = conv_silu(lambda s: cr_ref[s:s + q, :], cwc_ref[...], c)
        c_s[r0:r0 + q, :] = cconv
        cb_s[c] = _dot_nt(cconv.astype(BF16), bc.astype(BF16))
        raw =_dot_exact_rhs(dtr_ref[r0:r0 + q, :], pick) + dt_bias
        dtv = jnp.maximum(raw, 0.0) + jnp.log1p(jnp.exp(-jnp.abs(raw)))
        parts = _split3(dtv * a_neg)
        hi, mid, lo = parts
        acs = jnp.where(col_i < gh, _dot(tri_lo, hi) + _dot(tri_lo, mid) + _dot(tri_lo, lo),
                        _dot(tri_up, hi) + _dot(tri_up, mid) + _dot(tri_up, lo))
        acs_s[r0:r0 + q, :] = acs
        acst_s[c] = acs.T
        dtt_s[c] = dtv.T

    low_half = col_i < hp

    def scan_chunk(c, direction):
        st_s = (stf_s, stb_s)[direction]
        r0 = _aligned(c * q, q)
        rows = pl.ds(r0, q)
        xs = xs_s[rows, :]
        xb = xs.astype(BF16)
        cc = c_s[rows, :]
        acs = acs_s[rows, :]
        acst = acst_s[c]
        dtt = dtt_s[c]
        btf = bt_s[c]
        cb = cb_s[c]
        end = q - 1 if direction == 0 else 0
        tot = acst[:, end:end + 1]
        mine = (row_i >= gh * direction) & (row_i < gh * (direction + 1))
        w_out = dtt * jnp.exp(jnp.where(mine, tot - acst, 0.0))
        e_tot = jnp.exp(tot)
        keep = (row_i >= col_i) if direction == 0 else (col_i >= row_i)
        acs2 = acs * LOG2E
        src2 = (acst - jnp.log(dtt)) * LOG2E
        outs = []
        for pair in range(gh // 2):
            ms, ces, bws, ets = [], [], [], []
            for hh in range(2):
                col = gh * direction + 2 * pair + hh
                a_l = jnp.broadcast_to(acs2[:, col:col + 1], (q, q))
                decay = jnp.where(keep, jnp.exp2(a_l - src2[col:col + 1, :]), 0.0)
                ms.append((cb * decay).astype(BF16))
                ces.append((cc * jnp.exp2(a_l)).astype(BF16))
                bws.append((btf * w_out[col:col + 1, :]).astype(BF16))
                ets.append(e_tot[col:col + 1, :])
            slab = slice(pair * LANES, (pair + 1) * LANES)
            xp = xb[:, slab]
            zero = jnp.zeros_like(xp)
            x_bd = jnp.concatenate([jnp.where(low_half, xp, zero),
                                    jnp.where(low_half, zero, xp)], axis=0)
            st = st_s[:, slab]
            sb = st.astype(BF16)
            s_bd = jnp.concatenate([jnp.where(low_half, sb, zero),
                                    jnp.where(low_half, zero, sb)], axis=0)
            outs.append(_dot(jnp.concatenate(ms + ces, axis=1),
                             jnp.concatenate([x_bd, s_bd], axis=0)))
            grow = _dot(jnp.concatenate(bws, axis=1), x_bd)
            st_s[:, slab] = st * jnp.where(low_half, ets[0], ets[1]) + grow
        return rows, xs, jnp.concatenate(outs, axis=1)

    def finalize(rows, xs, y_scan):
        y = y_scan + dskip_ref[...] * xs
        y = y * _silu(z_ref[rows, :])
        ms = jnp.mean(y * y, axis=-1, keepdims=True)
        o_ref[rows, :] = (y * lax.rsqrt(ms + EPS) * nw_ref[...]).astype(BF16)

    stf_s[...] = jnp.zeros_like(stf_s)
    stb_s[...] = jnp.zeros_like(stb_s)

    def step(lo, hi, i, phase):
        rows_f, xs_f, yf = scan_chunk(lo + i, 0)
        if phase == "middle":
            _, _, yb = scan_chunk(lo + i, 1)
            finalize(rows_f, xs_f, yf + yb)
            return
        rows_b, xs_b, yb = scan_chunk(hi - 1 - i, 1)
        if phase == "park":
            y_s[rows_f, :] = yf
            y_s[rows_b, :] = yb
        else:
            finalize(rows_f, xs_f, yf + y_s[rows_f, :])
            finalize(rows_b, xs_b, y_s[rows_b, :] + yb)

    def segment(lo, hi):
        n = hi - lo
        half = n // 2
        if n <= 2:
            for i in range(n):
                step(lo, hi, i, "park" if i < half else ("middle" if 2 * i + 1 == n else "finish"))
            return

        def park(i, carry):
            step(lo, hi, i, "park")
            return carry

        def finish(i, carry):
            step(lo, hi, i, "finish")
            return carry

        lax.fori_loop(0, half, park, 0)
        if n % 2:
            step(lo, hi, half, "middle")
        lax.fori_loop(n - half, n, finish, 0)

    segment(0, n_ctx_chunks)
    segment(n_ctx_chunks, n_chunks)


def _ssd_call(xbc, dt, z, sp, ctx_len):
    bsz, _, seq, _ = xbc.shape
    q, gw, ng = SSM_CHUNK, SSM_GROUP_WIDTH, SSM_GROUPS
    n_chunks = seq // q
    xt = gw // LANES
    bt0 = SSM_D_INNER // LANES
    ct0 = (SSM_D_INNER + SSM_BC) // LANES
    grp = lambda r, w: pl.BlockSpec((None, r, w), lambda b, g: (g, 0, 0))
    kern = functools.partial(_ssd_kernel, n_chunks=n_chunks, n_ctx_chunks=ctx_len // q)
    return pl.pallas_call(
        kern,
        grid=(bsz, ng),
        in_specs=[
            pl.BlockSpec((None, xt, seq, LANES), lambda b, g: (b, g, 0, 0)),
            pl.BlockSpec((None, None, seq, LANES), lambda b, g: (b, bt0 + g, 0, 0)),
            pl.BlockSpec((None, None, seq, LANES), lambda b, g: (b, ct0 + g, 0, 0)),
            pl.BlockSpec((None, seq, DT_WIDTH), lambda b, g: (b, 0, 0)),
            pl.BlockSpec((None, seq, gw), lambda b, g: (b, 0, g)),
            grp(SUBLANES, gw), grp(SUBLANES, SSM_STATE), grp(SUBLANES, SSM_STATE),
            grp(1, DT_WIDTH), grp(1, DT_WIDTH), grp(1, gw), grp(1, gw),
        ],
        out_specs=pl.BlockSpec((None, seq, gw), lambda b, g: (b, 0, g)),
        out_shape=jax.ShapeDtypeStruct((bsz, seq, SSM_D_INNER), BF16),
        scratch_shapes=[
            pltpu.VMEM((seq, gw), F32),
            pltpu.VMEM((n_chunks, SSM_STATE, q), F32),
            pltpu.VMEM((seq, SSM_STATE), F32),
            pltpu.VMEM((n_chunks, q, q), F32),
            pltpu.VMEM((seq, DT_WIDTH), F32),
            pltpu.VMEM((n_chunks, DT_WIDTH, q), F32),
            pltpu.VMEM((n_chunks, DT_WIDTH, q), F32),
            pltpu.VMEM((seq, gw), F32),
            pltpu.VMEM((SSM_STATE, gw), F32),
            pltpu.VMEM((SSM_STATE, gw), F32),
        ],
        compiler_params=_params("parallel", "arbitrary"),
        name="ssd",
    )(xbc, xbc, xbc, dt, z, sp["cwx"], sp["cwb"], sp["cwc"], sp["dtb"], sp["alog"],
      sp["dskip"], sp["nw"])


def _aligned(start, multiple):
    return start if isinstance(start, int) else pl.multiple_of(start, multiple)


def _interleave(*stages):
    live = list(stages)
    while live:
        for st in list(live):
            if next(st, StopIteration) is StopIteration:
                live.remove(st)


def _gqa_kernel(q_ref, k_ref, vt_ref, o_ref, s_a, s_b, *, ctx_len, tq, skip_ctx):
    seq = k_ref.shape[0]
    kc = tq
    slots = (s_a, s_b)
    o_shift = ctx_len if skip_ctx else 0

    def scores(q_start, r, nk, out):
        q = q_ref[pl.ds(q_start, tq), r * GQA_HEAD_DIM:(r + 1) * GQA_HEAD_DIM]
        m = None
        for j in range(nk // kc):
            s = _dot_nt(k_ref[j * kc:(j + 1) * kc, :], q)
            slots[r % 2][j * kc:(j + 1) * kc, :] = s
            mj = jnp.max(s, axis=0, keepdims=True)
            m = mj if m is None else jnp.maximum(m, mj)
            yield
        out.append(m)

    def values(o_start, r, nk, m):
        l = None
        acc = None
        for j in range(nk // kc):
            p = jnp.exp2(slots[r % 2][j * kc:(j + 1) * kc, :] - m)
            lj = jnp.sum(p, axis=0, keepdims=True)
            l = lj if l is None else l + lj
            t = _dot(vt_ref[:, j * kc:(j + 1) * kc], p.astype(BF16))
            acc = t if acc is None else acc + t
            yield
        o_ref[pl.ds(o_start, tq), r * GQA_HEAD_DIM:(r + 1) * GQA_HEAD_DIM] = (
            (acc / l).T.astype(BF16))

    def tile(q_start, nk, m0, next_start):
        m = m0
        for r in range(GQA_GROUP):
            out = []
            stages = [values(_aligned(q_start - o_shift, tq), r, nk, m)]
            if r + 1 < GQA_GROUP:
                stages.append(scores(q_start, r + 1, nk, out))
            elif next_start is not None:
                stages.append(scores(next_start, 0, nk, out))
            _interleave(*stages)
            m = out[0] if out else None
        return m

    def first_scores(q_start, nk):
        out = []
        _interleave(scores(q_start, 0, nk, out))
        return out[0]

    if not skip_ctx:
        n_ctx = ctx_len // tq
        m = first_scores(0, ctx_len)
        for i in range(n_ctx):
            m = tile(i * tq, ctx_len, m, (i + 1) * tq if i + 1 < n_ctx else None)

    n_lat = (seq - ctx_len) // tq
    m = first_scores(ctx_len, seq)

    def body(i, m):
        start = pl.multiple_of(ctx_len + i * tq, tq)
        return tile(start, seq, m, pl.multiple_of(start + tq, tq))

    m = lax.fori_loop(0, n_lat - 1, body, m)
    tile(seq - tq, seq, m, None)


def _gqa_call(q, k, vt, ctx_len, tq, skip_ctx):
    bsz, seq, _ = q.shape
    gwid = GQA_GROUP * GQA_HEAD_DIM
    out_rows = seq - ctx_len if skip_ctx else seq
    kern = functools.partial(_gqa_kernel, ctx_len=ctx_len, tq=tq, skip_ctx=skip_ctx)
    return pl.pallas_call(
        kern,
        grid=(bsz, GQA_KV_HEADS),
        in_specs=[pl.BlockSpec((None, seq, gwid), lambda b, g: (b, 0, g)),
                  pl.BlockSpec((None, seq, GQA_HEAD_DIM), lambda b, g: (b, 0, g)),
                  pl.BlockSpec((None, GQA_HEAD_DIM, seq), lambda b, g: (b, g, 0))],
        out_specs=pl.BlockSpec((None, out_rows, gwid), lambda b, g: (b, 0, g)),
        out_shape=jax.ShapeDtypeStruct((bsz, out_rows, GQA_WIDTH), BF16),
        scratch_shapes=[pltpu.VMEM((seq, tq), F32), pltpu.VMEM((seq, tq), F32)],
        compiler_params=_params("parallel", "parallel"),
        name="gqa",
    )(q, k, vt)


def _diff_kernel(q_ref, k_ref, v_ref, lam_ref, nw_ref, o_ref, s_a, s_b, *, ctx_len, tq, skip_ctx,
                 lam_init):
    lv = lam_ref[...]
    lam = (jnp.exp(jnp.sum(lv[0:1] * lv[1:2], axis=-1, keepdims=True))
           - jnp.exp(jnp.sum(lv[2:3] * lv[3:4], axis=-1, keepdims=True)) + lam_init)
    seq = k_ref.shape[0]
    slots = (s_a, s_b)
    o_shift = ctx_len if skip_ctx else 0

    def scores(q_start, nk, slot):
        q = q_ref[pl.ds(q_start, tq), :]
        lane = lax.broadcasted_iota(jnp.int32, q.shape, 1)
        zero = jnp.zeros_like(q)
        q2 = jnp.concatenate([jnp.where(lane < DIFF_HEAD_DIM, q, zero),
                              jnp.where(lane >= DIFF_HEAD_DIM, q, zero)], axis=0)
        slots[slot][:, 0:nk] = _dot_nt(q2, k_ref[0:nk, :])

    def finish(nk, slot, o_start):
        s = slots[slot][:, 0:nk]
        p = jnp.exp2(s - jnp.max(s, axis=-1, keepdims=True))
        l = jnp.sum(p, axis=-1, keepdims=True)
        l1, l2 = l[0:tq], l[tq:2 * tq]
        w = p[0:tq] - p[tq:2 * tq] * (lam * l1 / l2)
        o = _dot(w.astype(BF16), v_ref[0:nk, :]) * (1.0 / l1)
        ms = jnp.mean(o * o, axis=-1, keepdims=True)
        o_ref[pl.ds(o_start, tq), :] = (
            o * lax.rsqrt(ms + EPS) * nw_ref[...] * (1.0 - lam_init)).astype(BF16)

    if not skip_ctx:
        for i in range(ctx_len // tq):
            scores(i * tq, ctx_len, 0)
            finish(ctx_len, 0, i * tq)

    def pair(t0, last):
        scores(_aligned(t0 + tq, tq), seq, 1)
        finish(seq, 0, _aligned(t0 - o_shift, tq))
        if not last:
            scores(_aligned(t0 + 2 * tq, tq), seq, 0)
        finish(seq, 1, _aligned(t0 + tq - o_shift, tq))

    n_pairs = (seq - ctx_len) // (2 * tq)
    scores(ctx_len, seq, 0)

    def body(j, carry):
        pair(ctx_len + 2 * j * tq, False)
        return carry

    lax.fori_loop(0, n_pairs - 1, body, 0)
    pair(ctx_len + 2 * (n_pairs - 1) * tq, True)


def _diff_call(q, k, v, lam_p, nw, lam_init, ctx_len, tq, skip_ctx):
    bsz, seq, _ = q.shape
    out_rows = seq - ctx_len if skip_ctx else seq
    kern = functools.partial(_diff_kernel, ctx_len=ctx_len, tq=tq, skip_ctx=skip_ctx,
                             lam_init=lam_init)
    head = pl.BlockSpec((None, seq, LANES), lambda b, h: (b, 0, h))
    return pl.pallas_call(
        kern,
        grid=(bsz, DIFF_HEADS),
        in_specs=[head, head, head, _const_spec(lam_p.shape), _const_spec(nw.shape)],
        out_specs=pl.BlockSpec((None, out_rows, LANES), lambda b, h: (b, 0, h)),
        out_shape=jax.ShapeDtypeStruct((bsz, out_rows, DIFF_WIDTH), BF16),
        scratch_shapes=[pltpu.VMEM((2 * tq, seq), F32), pltpu.VMEM((2 * tq, seq), F32)],
        compiler_params=_params("parallel", "parallel"),
        name="diff_attn",
    )(q, k, v, lam_p, nw)


def _merge_kernel(x_ref, mod_ref, ys_ref, yg_ref, yd_ref, wg_ref, bg_ref, ws_ref, wq_ref, wd_ref,
                  wo_ref, lg_ref, lb_ref, o_ref):
    d = D_MODEL
    x = x_ref[...]
    h = _modulated(x_ref, mod_ref, 0)
    m = None
    for i, (y_ref, w_ref) in enumerate(((ys_ref, ws_ref), (yg_ref, wq_ref), (yd_ref, wd_ref))):
        gate = _sigmoid(_dot(h, wg_ref[:, i * d:(i + 1) * d]) + bg_ref[:, i * d:(i + 1) * d])
        term = gate * _dot(y_ref[...], w_ref[...])
        m = term if m is None else m + term
    y = _dot(m.astype(BF16), wo_ref[...])
    r = DEEPNORM_ALPHA * x + mod_ref[2:3, :] * y
    o_ref[...] = _layer_norm(r, lg_ref[...], lb_ref[...])


def _merge_call(xs, mods, ys, yg, yd, lw, ctx_len, tm, row_off):
    bsz, seq, d = xs.shape
    nct = ctx_len // tm - row_off
    full = lambda w: pl.BlockSpec((None, tm, w), lambda b, t: (b, t + row_off, 0))
    tok = lambda w: pl.BlockSpec((None, tm, w), lambda b, t: (b, t, 0))
    consts = [lw["w_gate"], lw["b_gate"], lw["w_ssm_out"], lw["w_gqa_out"], lw["w_diff_out"],
              lw["w_o"], lw["ln1_g"], lw["ln1_b"]]
    return pl.pallas_call(
        _merge_kernel,
        grid=(bsz, seq // tm - row_off),
        in_specs=[full(d), pl.BlockSpec((None, 6, d), _mod_index(nct)),
                  full(SSM_D_INNER), tok(GQA_WIDTH), tok(DIFF_WIDTH)]
                 + [_const_spec(a.shape) for a in consts],
        out_specs=tok(d),
        out_shape=jax.ShapeDtypeStruct((bsz, seq - row_off * tm, d), F32),
        compiler_params=_params("parallel", "parallel"),
        name="merge",
    )(xs, mods, ys, yg, yd, *consts)


def _ffn_kernel(x_ref, mod_ref, wi_ref, wo_ref, lg_ref, lb_ref, o_ref):
    x = x_ref[...]
    h = _modulated(x_ref, mod_ref, 3)
    acc = None
    for c0 in range(0, FFN_HIDDEN, FFN_CHUNK):
        a = _dot(h, wi_ref[:, c0:c0 + FFN_CHUNK])
        b = _dot(h, wi_ref[:, FFN_HIDDEN + c0:FFN_HIDDEN + c0 + FFN_CHUNK])
        u = (_silu(a) * b).astype(BF16)
        part = _dot(u, wo_ref[c0:c0 + FFN_CHUNK, :])
        acc = part if acc is None else acc + part
    r = DEEPNORM_ALPHA * x + mod_ref[5:6, :] * acc
    o_ref[...] = _layer_norm(r, lg_ref[...], lb_ref[...])


def _ffn_call(xs, mods, lw, n_ctx_tiles, tm):
    bsz, seq, d = xs.shape
    tok = pl.BlockSpec((None, tm, d), lambda b, t: (b, t, 0))
    consts = [lw["ffn_w_in"], lw["ffn_w_out"], lw["ln2_g"], lw["ln2_b"]]
    return pl.pallas_call(
        _ffn_kernel,
        grid=(bsz, seq // tm),
        in_specs=[tok, pl.BlockSpec((None, 6, d), _mod_index(n_ctx_tiles))]
                 + [_const_spec(a.shape) for a in consts],
        out_specs=tok,
        out_shape=jax.ShapeDtypeStruct((bsz, seq, d), F32),
        compiler_params=_params("parallel", "parallel"),
        name="ffn",
    )(xs, mods, *consts)


def _rope_tables(ctx_len, lat_len, head_dim, half_sign_period):
    t = jnp.arange(lat_len, dtype=jnp.int32)
    d_axis = head_dim // 2
    inv_freq = ROPE_THETA ** (-jnp.arange(0, d_axis, 2, dtype=F32) / d_axis)
    ang_r = (t // GRID_W).astype(F32)[:, None] * inv_freq
    ang_c = (t % GRID_W).astype(F32)[:, None] * inv_freq
    ang = jnp.concatenate([ang_r, ang_r, ang_c, ang_c], axis=-1)
    cos, sin = jnp.cos(ang), jnp.sin(ang)
    reps = LANES // head_dim
    cos, sin = jnp.tile(cos, (1, reps)), jnp.tile(sin, (1, reps))
    lane = jnp.arange(LANES)
    sign = jnp.where((lane % half_sign_period) < half_sign_period // 2, -1.0, 1.0).astype(F32)
    cos = jnp.concatenate([jnp.ones((ctx_len, LANES), F32), cos], axis=0)
    sin = jnp.concatenate([jnp.zeros((ctx_len, LANES), F32), sin * sign], axis=0)
    return cos, sin


def _group_rows(v, width):
    return v.reshape(SSM_GROUPS, 1, width)


def _ssm_params(conv_w, conv_b, dt_bias, a_log, d_skip, norm_w):
    gh, ng = SSM_GROUP_HEADS, SSM_GROUPS

    def taps(lo, width):
        w = conv_w[:, lo:lo + ng * width].reshape(SSM_CONV, ng, width)
        b = conv_b[lo:lo + ng * width].reshape(1, ng, width)
        packed = jnp.concatenate(
            [w, b, jnp.zeros((SUBLANES - SSM_CONV - 1, ng, width), F32)], axis=0)
        return jnp.transpose(packed, (1, 0, 2))

    def per_dir(v):
        r = jnp.transpose(v.reshape(2, ng, gh), (1, 0, 2)).reshape(ng, 1, 2 * gh)
        return jnp.pad(r, ((0, 0), (0, 0), (0, DT_WIDTH - 2 * gh)))

    return {
        "cwx": taps(0, SSM_GROUP_WIDTH),
        "cwb": taps(SSM_D_INNER, SSM_STATE),
        "cwc": taps(SSM_D_INNER + SSM_BC, SSM_STATE),
        "dtb": per_dir(dt_bias), "alog": per_dir(a_log),
        "dskip": _group_rows(jnp.repeat(d_skip, SSM_HEAD_DIM), SSM_GROUP_WIDTH),
        "nw": _group_rows(norm_w, SSM_GROUP_WIDTH),
    }


def kernel(x, c, ctx, c_ctx, ada_w, ada_b, w_in, b_gate, ssm_conv_w, ssm_conv_b, ssm_dt_bias,
           ssm_a_log, ssm_d, ssm_norm_w, w_ssm_out, gqa_q_norm, gqa_k_norm, w_gqa_out, diff_lambda,
           diff_norm_w, w_diff_out, w_o, ln1_g, ln1_b, ffn_w_in, ffn_w_out, ln2_g, ln2_b):
    bsz, lat_len, d = x.shape
    ctx_len = ctx.shape[1]
    depth = w_in.shape[0]
    assert d == D_MODEL and depth == DEPTH and w_in.shape[2] == IN_WIDTH
    assert lat_len % GRID_W == 0 and lat_len % SSM_CHUNK == 0 and ctx_len % SSM_CHUNK == 0
    tm = math.gcd(256, ctx_len)
    tq = tm

    rows = -(-(bsz + 1) // SUBLANES) * SUBLANES
    cvec = jnp.concatenate([c, c_ctx[None], jnp.zeros((rows - bsz - 1, d), F32)], axis=0)
    ada = _ada_call(cvec, ada_w, ada_b)

    rope_g = _rope_tables(ctx_len, lat_len, GQA_HEAD_DIM, GQA_HEAD_DIM // 2)
    rope_d = _rope_tables(ctx_len, lat_len, DIFF_HEAD_DIM, DIFF_HEAD_DIM // 2)
    tabs = (*rope_g, *rope_d)

    xs = jnp.concatenate([ctx, x], axis=1)
    pad = jnp.zeros((d, DT_WIDTH - 2 * SSM_HEADS), BF16)
    for i in range(depth):
        need_ctx = i < depth - 1
        lam_init = 0.8 - 0.6 * math.exp(-0.3 * i)
        mod_l = ada[i, :bsz].reshape(bsz, 1, 6, d)
        mod_c = jnp.broadcast_to(ada[i, bsz].reshape(1, 1, 6, d), (bsz, 1, 6, d))
        mods = jnp.concatenate([mod_c, mod_l], axis=1).reshape(2 * bsz, 6, d)
        w = w_in[i].astype(BF16)
        w_ssm = jnp.concatenate([w[:, :OFF_GQ], pad], axis=1)
        w_attn = w[:, OFF_GQ:OFF_GATE]
        lw = {
            "w_gate": w[:, OFF_GATE:], "b_gate": b_gate[i][None],
            "w_ssm_out": w_ssm_out[i].astype(BF16), "w_gqa_out": w_gqa_out[i].astype(BF16),
            "w_diff_out": w_diff_out[i].astype(BF16), "w_o": w_o[i].astype(BF16),
            "ln1_g": ln1_g[i][None], "ln1_b": ln1_b[i][None],
            "ffn_w_in": ffn_w_in[i].astype(BF16), "ffn_w_out": ffn_w_out[i].astype(BF16),
            "ln2_g": ln2_g[i][None], "ln2_b": ln2_b[i][None],
        }
        sp = _ssm_params(ssm_conv_w[i], ssm_conv_b[i], ssm_dt_bias[i], ssm_a_log[i], ssm_d[i],
                         ssm_norm_w[i])

        z, xbc, dt = _inproj_ssm_call(xs, mods, w_ssm, ctx_len, tm)
        gq, gk, gvt, dq, dk, dv = _inproj_attn_call(
            xs, mods, w_attn, gqa_q_norm[i][None], gqa_k_norm[i][None], tabs, ctx_len, tm)
        y_ssm = _ssd_call(xbc, dt, z, sp, ctx_len)
        off = 0 if need_ctx else ctx_len // tm
        y_gqa = _gqa_call(gq, gk, gvt, ctx_len, tq, not need_ctx)
        y_diff = _diff_call(dq, dk, dv, diff_lambda[i], diff_norm_w[i][None], lam_init, ctx_len, tq,
                            not need_ctx)
        x1 = _merge_call(xs, mods, y_ssm, y_gqa, y_diff, lw, ctx_len, tm, off)
        xs = _ffn_call(x1, mods, lw, ctx_len // tm - off, tm)
    return xs
```

```python
import functools
import math

import jax
import jax.numpy as jnp
from jax import lax
from jax.experimental import pallas as pl
from jax.experimental.pallas import tpu as pltpu

F32 = jnp.float32
BF16 = jnp.bfloat16

D_MODEL = 1024
DEPTH = 4
GRID_W = 64
ROPE_THETA = 10000.0
EPS = 1e-6

SSM_D_INNER = 2048
SSM_HEAD_DIM = 64
SSM_HEADS = 32
SSM_GROUPS = 4
SSM_STATE = 128
SSM_CONV = 5
SSM_CHUNK = 128
SSM_BC = SSM_GROUPS * SSM_STATE
SSM_CONV_DIM = SSM_D_INNER + 2 * SSM_BC
SSM_GROUP_WIDTH = SSM_D_INNER // SSM_GROUPS
SSM_GROUP_HEADS = SSM_HEADS // SSM_GROUPS

GQA_HEAD_DIM = 128
GQA_HEADS = 8
GQA_KV_HEADS = 2
GQA_GROUP = GQA_HEADS // GQA_KV_HEADS
GQA_WIDTH = GQA_HEADS * GQA_HEAD_DIM
GQA_KV_WIDTH = GQA_KV_HEADS * GQA_HEAD_DIM

DIFF_HEAD_DIM = 64
DIFF_HEADS = 8
DIFF_WIDTH = DIFF_HEADS * 2 * DIFF_HEAD_DIM

N_BRANCHES = 3
FFN_HIDDEN = 2816
FFN_CHUNK = 1408

DEEPNORM_ALPHA = (2 * DEPTH) ** 0.25

LANES = 128
SUBLANES = 8
DT_WIDTH = LANES
SSM_CONV_TILES = SSM_CONV_DIM // LANES
LOG2E = 1.4426950408889634
VMEM_LIMIT = 56 * 1024 * 1024
MAX_ROW_TILE = 768

OFF_Z = 0
OFF_XBC = OFF_Z + SSM_D_INNER
OFF_DT = OFF_XBC + SSM_CONV_DIM
OFF_GQ = OFF_DT + 2 * SSM_HEADS
OFF_GK = OFF_GQ + GQA_WIDTH
OFF_GV = OFF_GK + GQA_KV_WIDTH
OFF_DQ = OFF_GV + GQA_KV_WIDTH
OFF_DK = OFF_DQ + DIFF_WIDTH
OFF_DV = OFF_DK + DIFF_WIDTH
OFF_GATE = OFF_DV + DIFF_WIDTH
IN_WIDTH = OFF_GATE + N_BRANCHES * D_MODEL


def _dot(a, b):
    return jnp.dot(a, b, preferred_element_type=F32)


def _dot_nt(a, b):
    return lax.dot_general(a, b, (((1,), (1,)), ((), ())), preferred_element_type=F32)


def _split3(a):
    hi = a.astype(BF16)
    r = a - hi.astype(F32)
    mid = r.astype(BF16)
    lo = (r - mid.astype(F32)).astype(BF16)
    return hi, mid, lo


def _dot_exact_rhs(a, sel):
    hi, mid, lo = _split3(a)
    return _dot(hi, sel) + _dot(mid, sel) + _dot(lo, sel)


def _sigmoid(x):
    return 1.0 / (1.0 + jnp.exp(-x))


def _silu(x):
    return x * _sigmoid(x)


def _layer_norm(r, g, b):
    mu = jnp.mean(r, axis=-1, keepdims=True)
    d = r - mu
    var = jnp.mean(d * d, axis=-1, keepdims=True)
    return d * lax.rsqrt(var + EPS) * g + b


def _params(*sem):
    return pltpu.CompilerParams(dimension_semantics=sem, vmem_limit_bytes=VMEM_LIMIT)


def _const_spec(shape):
    nd = len(shape)
    return pl.BlockSpec(shape, lambda *_: (0,) * nd, pipeline_mode=pl.Buffered(1))


def _ada_kernel(c_ref, w_ref, b_ref, o_ref):
    sc = _silu(c_ref[...]).astype(BF16)
    o_ref[...] = _dot(sc, w_ref[...].astype(BF16)) + b_ref[...]


def _ada_call(cvec, ada_w, ada_b):
    rows = cvec.shape[0]
    depth, d, n = ada_w.shape
    tn = n // 4
    return pl.pallas_call(
        _ada_kernel,
        grid=(depth, n // tn),
        in_specs=[
            pl.BlockSpec((rows, d), lambda i, j: (0, 0)),
            pl.BlockSpec((None, d, tn), lambda i, j: (i, 0, j)),
            pl.BlockSpec((None, 1, tn), lambda i, j: (i, 0, j)),
        ],
        out_specs=pl.BlockSpec((None, rows, tn), lambda i, j: (i, 0, j)),
        out_shape=jax.ShapeDtypeStruct((depth, rows, n), F32),
        compiler_params=_params("arbitrary", "arbitrary"),
        name="ada",
    )(cvec, ada_w, ada_b.reshape(depth, 1, n))


def _mod_row(mod_ref, row, n_ctx_rows, tm):
    lat = mod_ref[1, row:row + 1, :]
    if n_ctx_rows == 0:
        return lat
    first = pl.program_id(1) * tm
    is_ctx = first + lax.broadcasted_iota(jnp.int32, (tm, 1), 0) < n_ctx_rows
    return jnp.where(is_ctx, mod_ref[0, row:row + 1, :], lat)


def _modulated(x_ref, mod_ref, shift_row, n_ctx_rows):
    x = x_ref[...]
    shift = _mod_row(mod_ref, shift_row, n_ctx_rows, x.shape[0])
    scale = _mod_row(mod_ref, shift_row + 1, n_ctx_rows, x.shape[0])
    return (x * (1.0 + scale) + shift).astype(BF16)


def _inproj_ssm_kernel(x_ref, mod_ref, w_ref, z_ref, xbc_ref, dt_ref, *, n_ctx_rows):
    h = _modulated(x_ref, mod_ref, 0, n_ctx_rows)
    step = 512
    for c0 in range(0, SSM_D_INNER, step):
        z_ref[:, c0:c0 + step] = _dot(h, w_ref[:, OFF_Z + c0:OFF_Z + c0 + step])
    for c0 in range(0, SSM_CONV_DIM, step):
        u = _dot(h, w_ref[:, OFF_XBC + c0:OFF_XBC + c0 + step])
        for i in range(step // LANES):
            xbc_ref[c0 // LANES + i] = u[:, i * LANES:(i + 1) * LANES]
    dt_ref[...] = _dot(h, w_ref[:, OFF_DT:OFF_DT + DT_WIDTH])


def _mod_spec(d):
    return pl.BlockSpec((None, 2, 6, d), lambda b, t: (b, 0, 0, 0))


def _row_tile(rows, unit):
    best = unit
    for k in range(1, MAX_ROW_TILE // unit + 1):
        if rows % (k * unit) == 0:
            best = k * unit
    return best


def _inproj_ssm_call(xs, mods, w_ssm, ctx_len, tm):
    bsz, seq, d = xs.shape
    tok = lambda w: pl.BlockSpec((None, tm, w), lambda b, t: (b, t, 0))
    return pl.pallas_call(
        functools.partial(_inproj_ssm_kernel, n_ctx_rows=ctx_len),
        grid=(bsz, seq // tm),
        in_specs=[tok(d), _mod_spec(d), _const_spec(w_ssm.shape)],
        out_specs=[tok(SSM_D_INNER),
                   pl.BlockSpec((None, SSM_CONV_TILES, tm, LANES), lambda b, t: (b, 0, t, 0)),
                   tok(DT_WIDTH)],
        out_shape=[jax.ShapeDtypeStruct((bsz, seq, SSM_D_INNER), F32),
                   jax.ShapeDtypeStruct((bsz, SSM_CONV_TILES, seq, LANES), F32),
                   jax.ShapeDtypeStruct((bsz, seq, DT_WIDTH), F32)],
        compiler_params=_params("parallel", "parallel"),
        name="inproj_ssm",
    )(xs, mods, w_ssm)


def _rope(u, cos, sin_signed, half):
    lane = lax.broadcasted_iota(jnp.int32, u.shape, 1)
    fwd = pltpu.roll(u, LANES - half, axis=1)
    bwd = pltpu.roll(u, half, axis=1)
    partner = jnp.where((lane % (2 * half)) < half, fwd, bwd)
    return u * cos + partner * sin_signed


def _rms_heads(u, g):
    ms = jnp.mean(u * u, axis=-1, keepdims=True)
    return u * lax.rsqrt(ms + EPS) * g


def _inproj_attn_kernel(x_ref, mod_ref, w_ref, qn_ref, kn_ref, cg_ref, sg_ref, cd_ref, sd_ref,
                        gq_ref, gk_ref, gv_ref, dq_ref, dk_ref, dv_ref, *, n_ctx_rows):
    h = _modulated(x_ref, mod_ref, 0, n_ctx_rows)
    base = OFF_GQ
    cg, sg = cg_ref[...], sg_ref[...]
    cd, sd = cd_ref[...], sd_ref[...]
    hd = GQA_HEAD_DIM

    def heads(off, width):
        step = min(512, width)
        for c0 in range(0, width, step):
            u = _dot(h, w_ref[:, off - base + c0:off - base + c0 + step])
            for j in range(step // LANES):
                yield (c0 // LANES + j) * LANES, u[:, j * LANES:(j + 1) * LANES]

    gqa_scale = GQA_HEAD_DIM ** -0.5 * LOG2E
    for c, u in heads(OFF_GQ, GQA_WIDTH):
        u = _rope(_rms_heads(u, qn_ref[...]), cg, sg, GQA_HEAD_DIM // 4)
        gq_ref[:, c:c + hd] = (u * gqa_scale).astype(BF16)
    for c, u in heads(OFF_GK, GQA_KV_WIDTH):
        u = _rope(_rms_heads(u, kn_ref[...]), cg, sg, GQA_HEAD_DIM // 4)
        gk_ref[:, c:c + hd] = u.astype(BF16)
    for c, u in heads(OFF_GV, GQA_KV_WIDTH):
        gv_ref[c:c + hd, :] = u.T.astype(BF16)
    scale = DIFF_HEAD_DIM ** -0.5 * LOG2E
    for c, u in heads(OFF_DQ, DIFF_WIDTH):
        dq_ref[:, c:c + LANES] = (_rope(u, cd, sd, DIFF_HEAD_DIM // 4) * scale).astype(BF16)
    for c, u in heads(OFF_DK, DIFF_WIDTH):
        dk_ref[:, c:c + LANES] = _rope(u, cd, sd, DIFF_HEAD_DIM // 4).astype(BF16)
    for c0 in range(0, DIFF_WIDTH, 512):
        dv_ref[:, c0:c0 + 512] = _dot(
            h, w_ref[:, OFF_DV - base + c0:OFF_DV - base + c0 + 512]).astype(BF16)


def _inproj_attn_call(xs, mods, w_attn, qn, kn, tabs, ctx_len, tm):
    bsz, seq, d = xs.shape
    tok = lambda w: pl.BlockSpec((None, tm, w), lambda b, t: (b, t, 0))
    tab = pl.BlockSpec((tm, LANES), lambda b, t: (t, 0))
    tr = lambda w: pl.BlockSpec((None, w, tm), lambda b, t: (b, 0, t))
    row = lambda w: jax.ShapeDtypeStruct((bsz, seq, w), BF16)
    col = lambda w: jax.ShapeDtypeStruct((bsz, w, seq), BF16)
    return pl.pallas_call(
        functools.partial(_inproj_attn_kernel, n_ctx_rows=ctx_len),
        grid=(bsz, seq // tm),
        in_specs=[tok(d), _mod_spec(d), _const_spec(w_attn.shape),
                  _const_spec(qn.shape), _const_spec(kn.shape), tab, tab, tab, tab],
        out_specs=[tok(GQA_WIDTH), tok(GQA_KV_WIDTH), tr(GQA_KV_WIDTH),
                   tok(DIFF_WIDTH), tok(DIFF_WIDTH), tok(DIFF_WIDTH)],
        out_shape=[row(GQA_WIDTH), row(GQA_KV_WIDTH), col(GQA_KV_WIDTH),
                   row(DIFF_WIDTH), row(DIFF_WIDTH), row(DIFF_WIDTH)],
        compiler_params=_params("parallel", "parallel"),
        name="inproj_attn",
    )(xs, mods, w_attn, qn, kn, *tabs)


def _ssd_kernel(xr_ref, br_ref, cr_ref, dtr_ref, z_ref, cwx_ref, cwb_ref, cwc_ref,
                dtb_ref, alog_ref, dskip_ref, nw_ref, o_ref,
                xs_s, bt_s, c_s, cb_s, acs_s, acst_s, dtt_s, y_s, stf_s, stb_s, *,
                n_chunks, n_ctx_chunks):
    q = SSM_CHUNK
    gh = SSM_GROUP_HEADS
    hp = SSM_HEAD_DIM
    seq = n_chunks * q
    g = pl.program_id(1)

    row_i = lax.broadcasted_iota(jnp.int32, (q, q), 0)
    col_i = lax.broadcasted_iota(jnp.int32, (q, q), 1)
    tri_lo = (col_i <= row_i).astype(BF16)
    tri_up = (col_i >= row_i).astype(BF16)
    src = jnp.where(col_i < gh, g * gh + col_i, SSM_HEADS + g * gh + col_i - gh)
    pick = ((row_i == src) & (col_i < 2 * gh)).astype(BF16)
    a_neg = -jnp.exp(alog_ref[...])
    dt_bias = dtb_ref[...]

    def conv_silu(load, cw, c):
        r0 = c * q
        acc = None
        for k in range(SSM_CONV):
            off = k - SSM_CONV // 2
            start = r0 + off
            valid = None
            if start < 0:
                t = pltpu.roll(load(0), -off, axis=0)
                valid = row_i >= -off
            elif start + q > seq:
                t = pltpu.roll(load(seq - q), (q - off) % q, axis=0)
                valid = row_i < q - off
            else:
                t = load(start)
            if c == n_ctx_chunks and off < 0:
                valid = row_i >= -off
            if c == n_ctx_chunks - 1 and off > 0:
                valid = row_i < q - off
            if valid is not None:
                t = jnp.where(valid, t, 0.0)
            term = t * cw[k:k + 1, :]
            acc = term if acc is None else acc + term
        return _silu(acc + cw[SSM_CONV:SSM_CONV + 1, :])

    for c in range(n_chunks):
        r0 = c * q
        for j in range(SSM_GROUP_WIDTH // LANES):
            cols = slice(j * LANES, (j + 1) * LANES)
            xs_s[r0:r0 + q, cols] = conv_silu(
                lambda s, j=j: xr_ref[j, s:s + q, :], cwx_ref[:, cols], c)
        bc = conv_silu(lambda s: br_ref[s:s + q, :], cwb_ref[...], c)
        bt_s[c] = bc.T
        cconv = conv_silu(lambda s: cr_ref[s:s + q, :], cwc_ref[...], c)
        c_s[r0:r0 + q, :] = cconv
        cb_s[c] = _dot_nt(cconv.astype(BF16), bc.astype(BF16))
        raw =_dot_exact_rhs(dtr_ref[r0:r0 + q, :], pick) + dt_bias
        dtv = jnp.maximum(raw, 0.0) + jnp.log1p(jnp.exp(-jnp.abs(raw)))
        parts = _split3(dtv * a_neg)
        hi, mid, lo = parts
        acs = jnp.where(col_i < gh, _dot(tri_lo, hi) + _dot(tri_lo, mid) + _dot(tri_lo, lo),
                        _dot(tri_up, hi) + _dot(tri_up, mid) + _dot(tri_up, lo))
        acs_s[r0:r0 + q, :] = acs
        acst_s[c] = acs.T
        dtt_s[c] = dtv.T

    low_half = col_i < hp

    def scan_chunk(c, direction):
        st_s = (stf_s, stb_s)[direction]
        r0 = _aligned(c * q, q)
        rows = pl.ds(r0, q)
        xs = xs_s[rows, :]
        xb = xs.astype(BF16)
        cc = c_s[rows, :]
        acs = acs_s[rows, :]
        acst = acst_s[c]
        dtt = dtt_s[c]
        btf = bt_s[c]
        cb = cb_s[c]
        end = q - 1 if direction == 0 else 0
        tot = acst[:, end:end + 1]
        mine = (row_i >= gh * direction) & (row_i < gh * (direction + 1))
        w_out = dtt * jnp.exp(jnp.where(mine, tot - acst, 0.0))
        e_tot = jnp.exp(tot)
        keep = (row_i >= col_i) if direction == 0 else (col_i >= row_i)
        acs2 = acs * LOG2E
        src2 = (acst - jnp.log(dtt)) * LOG2E
        outs = []
        for pair in range(gh // 2):
            ms, ces, bws, ets = [], [], [], []
            for hh in range(2):
                col = gh * direction + 2 * pair + hh
                a_l = jnp.broadcast_to(acs2[:, col:col + 1], (q, q))
                decay = jnp.where(keep, jnp.exp2(a_l - src2[col:col + 1, :]), 0.0)
                ms.append((cb * decay).astype(BF16))
                ces.append((cc * jnp.exp2(a_l)).astype(BF16))
                bws.append((btf * w_out[col:col + 1, :]).astype(BF16))
                ets.append(e_tot[col:col + 1, :])
            slab = slice(pair * LANES, (pair + 1) * LANES)
            xp = xb[:, slab]
            zero = jnp.zeros_like(xp)
            x_bd = jnp.concatenate([jnp.where(low_half, xp, zero),
                                    jnp.where(low_half, zero, xp)], axis=0)
            st = st_s[:, slab]
            sb = st.astype(BF16)
            s_bd = jnp.concatenate([jnp.where(low_half, sb, zero),
                                    jnp.where(low_half, zero, sb)], axis=0)
            outs.append(_dot(jnp.concatenate(ms + ces, axis=1),
                             jnp.concatenate([x_bd, s_bd], axis=0)))
            grow = _dot(jnp.concatenate(bws, axis=1), x_bd)
            st_s[:, slab] = st * jnp.where(low_half, ets[0], ets[1]) + grow
        return rows, xs, jnp.concatenate(outs, axis=1)

    def finalize(rows, xs, y_scan):
        y = y_scan + dskip_ref[...] * xs
        y = y * _silu(z_ref[rows, :])
        ms = jnp.mean(y * y, axis=-1, keepdims=True)
        o_ref[rows, :] = (y * lax.rsqrt(ms + EPS) * nw_ref[...]).astype(BF16)

    stf_s[...] = jnp.zeros_like(stf_s)
    stb_s[...] = jnp.zeros_like(stb_s)

    def step(lo, hi, i, phase):
        rows_f, xs_f, yf = scan_chunk(lo + i, 0)
        if phase == "middle":
            _, _, yb = scan_chunk(lo + i, 1)
            finalize(rows_f, xs_f, yf + yb)
            return
        rows_b, xs_b, yb = scan_chunk(hi - 1 - i, 1)
        if phase == "park":
            y_s[rows_f, :] = yf
            y_s[rows_b, :] = yb
        else:
            finalize(rows_f, xs_f, yf + y_s[rows_f, :])
            finalize(rows_b, xs_b, y_s[rows_b, :] + yb)

    def segment(lo, hi):
        n = hi - lo
        half = n // 2
        if n <= 2:
            for i in range(n):
                step(lo, hi, i, "park" if i < half else ("middle" if 2 * i + 1 == n else "finish"))
            return

        def park(i, carry):
            step(lo, hi, i, "park")
            return carry

        def finish(i, carry):
            step(lo, hi, i, "finish")
            return carry

        lax.fori_loop(0, half, park, 0, unroll=2)
        if n % 2:
            step(lo, hi, half, "middle")
        lax.fori_loop(n - half, n, finish, 0, unroll=2)

    segment(0, n_ctx_chunks)
    segment(n_ctx_chunks, n_chunks)


def _ssd_call(xbc, dt, z, sp, ctx_len):
    bsz, _, seq, _ = xbc.shape
    q, gw, ng = SSM_CHUNK, SSM_GROUP_WIDTH, SSM_GROUPS
    n_chunks = seq // q
    xt = gw // LANES
    bt0 = SSM_D_INNER // LANES
    ct0 = (SSM_D_INNER + SSM_BC) // LANES
    grp = lambda r, w: pl.BlockSpec((None, r, w), lambda b, g: (g, 0, 0))
    kern = functools.partial(_ssd_kernel, n_chunks=n_chunks, n_ctx_chunks=ctx_len // q)
    return pl.pallas_call(
        kern,
        grid=(bsz, ng),
        in_specs=[
            pl.BlockSpec((None, xt, seq, LANES), lambda b, g: (b, g, 0, 0)),
            pl.BlockSpec((None, None, seq, LANES), lambda b, g: (b, bt0 + g, 0, 0)),
            pl.BlockSpec((None, None, seq, LANES), lambda b, g: (b, ct0 + g, 0, 0)),
            pl.BlockSpec((None, seq, DT_WIDTH), lambda b, g: (b, 0, 0)),
            pl.BlockSpec((None, seq, gw), lambda b, g: (b, 0, g)),
            grp(SUBLANES, gw), grp(SUBLANES, SSM_STATE), grp(SUBLANES, SSM_STATE),
            grp(1, DT_WIDTH), grp(1, DT_WIDTH), grp(1, gw), grp(1, gw),
        ],
        out_specs=pl.BlockSpec((None, seq, gw), lambda b, g: (b, 0, g)),
        out_shape=jax.ShapeDtypeStruct((bsz, seq, SSM_D_INNER), BF16),
        scratch_shapes=[
            pltpu.VMEM((seq, gw), F32),
            pltpu.VMEM((n_chunks, SSM_STATE, q), F32),
            pltpu.VMEM((seq, SSM_STATE), F32),
            pltpu.VMEM((n_chunks, q, q), F32),
            pltpu.VMEM((seq, DT_WIDTH), F32),
            pltpu.VMEM((n_chunks, DT_WIDTH, q), F32),
            pltpu.VMEM((n_chunks, DT_WIDTH, q), F32),
            pltpu.VMEM((seq, gw), F32),
            pltpu.VMEM((SSM_STATE, gw), F32),
            pltpu.VMEM((SSM_STATE, gw), F32),
        ],
        compiler_params=_params("parallel", "arbitrary"),
        name="ssd",
    )(xbc, xbc, xbc, dt, z, sp["cwx"], sp["cwb"], sp["cwc"], sp["dtb"], sp["alog"],
      sp["dskip"], sp["nw"])


def _aligned(start, multiple):
    return start if isinstance(start, int) else pl.multiple_of(start, multiple)


def _interleave(*stages):
    live = list(stages)
    while live:
        for st in list(live):
            if next(st, StopIteration) is StopIteration:
                live.remove(st)


def _gqa_kernel(q_ref, k_ref, vt_ref, o_ref, s_a, s_b, *, ctx_len, tq, skip_ctx):
    seq = k_ref.shape[0]
    kc = tq
    slots = (s_a, s_b)
    o_shift = ctx_len if skip_ctx else 0

    def scores(q_start, r, nk, out):
        q = q_ref[pl.ds(q_start, tq), r * GQA_HEAD_DIM:(r + 1) * GQA_HEAD_DIM]
        m = None
        for j in range(nk // kc):
            s = _dot_nt(k_ref[j * kc:(j + 1) * kc, :], q)
            slots[r % 2][j * kc:(j + 1) * kc, :] = s
            mj = jnp.max(s, axis=0, keepdims=True)
            m = mj if m is None else jnp.maximum(m, mj)
            yield
        out.append(m)

    def values(o_start, r, nk, m):
        l = None
        acc = None
        for j in range(nk // kc):
            p = jnp.exp2(slots[r % 2][j * kc:(j + 1) * kc, :] - m)
            lj = jnp.sum(p, axis=0, keepdims=True)
            l = lj if l is None else l + lj
            t = _dot(vt_ref[:, j * kc:(j + 1) * kc], p.astype(BF16))
            acc = t if acc is None else acc + t
            yield
        o_ref[pl.ds(o_start, tq), r * GQA_HEAD_DIM:(r + 1) * GQA_HEAD_DIM] = (
            (acc / l).T.astype(BF16))

    def tile(q_start, nk, m0, next_start):
        m = m0
        for r in range(GQA_GROUP):
            out = []
            stages = [values(_aligned(q_start - o_shift, tq), r, nk, m)]
            if r + 1 < GQA_GROUP:
                stages.append(scores(q_start, r + 1, nk, out))
            elif next_start is not None:
                stages.append(scores(next_start, 0, nk, out))
            _interleave(*stages)
            m = out[0] if out else None
        return m

    def first_scores(q_start, nk):
        out = []
        _interleave(scores(q_start, 0, nk, out))
        return out[0]

    if not skip_ctx:
        n_ctx = ctx_len // tq
        m = first_scores(0, ctx_len)
        for i in range(n_ctx):
            m = tile(i * tq, ctx_len, m, (i + 1) * tq if i + 1 < n_ctx else None)

    n_lat = (seq - ctx_len) // tq
    m = first_scores(ctx_len, seq)

    def body(i, m):
        start = pl.multiple_of(ctx_len + i * tq, tq)
        return tile(start, seq, m, pl.multiple_of(start + tq, tq))

    m = lax.fori_loop(0, n_lat - 1, body, m)
    tile(seq - tq, seq, m, None)


def _gqa_call(q, k, vt, ctx_len, tq, skip_ctx):
    bsz, seq, _ = q.shape
    gwid = GQA_GROUP * GQA_HEAD_DIM
    out_rows = seq - ctx_len if skip_ctx else seq
    kern = functools.partial(_gqa_kernel, ctx_len=ctx_len, tq=tq, skip_ctx=skip_ctx)
    return pl.pallas_call(
        kern,
        grid=(bsz, GQA_KV_HEADS),
        in_specs=[pl.BlockSpec((None, seq, gwid), lambda b, g: (b, 0, g)),
                  pl.BlockSpec((None, seq, GQA_HEAD_DIM), lambda b, g: (b, 0, g)),
                  pl.BlockSpec((None, GQA_HEAD_DIM, seq), lambda b, g: (b, g, 0))],
        out_specs=pl.BlockSpec((None, out_rows, gwid), lambda b, g: (b, 0, g)),
        out_shape=jax.ShapeDtypeStruct((bsz, out_rows, GQA_WIDTH), BF16),
        scratch_shapes=[pltpu.VMEM((seq, tq), F32), pltpu.VMEM((seq, tq), F32)],
        compiler_params=_params("parallel", "parallel"),
        name="gqa",
    )(q, k, vt)


def _diff_kernel(q_ref, k_ref, v_ref, lam_ref, nw_ref, o_ref, s_a, s_b, m_a, m_b, *, ctx_len, tq,
                 skip_ctx, lam_init):
    lv = lam_ref[...]
    lam = (jnp.exp(jnp.sum(lv[0:1] * lv[1:2], axis=-1, keepdims=True))
           - jnp.exp(jnp.sum(lv[2:3] * lv[3:4], axis=-1, keepdims=True)) + lam_init)
    seq = k_ref.shape[0]
    slots = (s_a, s_b)
    maxes = (m_a, m_b)
    o_shift = ctx_len if skip_ctx else 0

    def scores(q_start, nk, slot):
        q = q_ref[pl.ds(q_start, tq), :]
        lane = lax.broadcasted_iota(jnp.int32, q.shape, 1)
        zero = jnp.zeros_like(q)
        q2 = jnp.concatenate([jnp.where(lane < DIFF_HEAD_DIM, q, zero),
                              jnp.where(lane >= DIFF_HEAD_DIM, q, zero)], axis=0)
        s = _dot_nt(q2, k_ref[0:nk, :])
        slots[slot][:, 0:nk] = s
        maxes[slot][...] = jnp.max(s, axis=-1, keepdims=True)

    def finish(nk, slot, o_start):
        s = slots[slot][:, 0:nk]
        p = jnp.exp2(s - maxes[slot][...])
        l = jnp.sum(p, axis=-1, keepdims=True)
        l1, l2 = l[0:tq], l[tq:2 * tq]
        w = p[0:tq] - p[tq:2 * tq] * (lam * l1 / l2)
        o = _dot(w.astype(BF16), v_ref[0:nk, :]) * (1.0 / l1)
        ms = jnp.mean(o * o, axis=-1, keepdims=True)
        o_ref[pl.ds(o_start, tq), :] = (
            o * lax.rsqrt(ms + EPS) * nw_ref[...] * (1.0 - lam_init)).astype(BF16)

    if not skip_ctx:
        for i in range(ctx_len // tq):
            scores(i * tq, ctx_len, 0)
            finish(ctx_len, 0, i * tq)

    def pair(t0, last):
        scores(_aligned(t0 + tq, tq), seq, 1)
        finish(seq, 0, _aligned(t0 - o_shift, tq))
        if not last:
            scores(_aligned(t0 + 2 * tq, tq), seq, 0)
        finish(seq, 1, _aligned(t0 + tq - o_shift, tq))

    n_pairs = (seq - ctx_len) // (2 * tq)
    scores(ctx_len, seq, 0)
    for j in range(n_pairs):
        pair(ctx_len + 2 * j * tq, j == n_pairs - 1)


def _diff_call(q, k, v, lam_p, nw, lam_init, ctx_len, tq, skip_ctx):
    bsz, seq, _ = q.shape
    out_rows = seq - ctx_len if skip_ctx else seq
    kern = functools.partial(_diff_kernel, ctx_len=ctx_len, tq=tq, skip_ctx=skip_ctx,
                             lam_init=lam_init)
    head = pl.BlockSpec((None, seq, LANES), lambda b, h: (b, 0, h))
    return pl.pallas_call(
        kern,
        grid=(bsz, DIFF_HEADS),
        in_specs=[head, head, head, _const_spec(lam_p.shape), _const_spec(nw.shape)],
        out_specs=pl.BlockSpec((None, out_rows, LANES), lambda b, h: (b, 0, h)),
        out_shape=jax.ShapeDtypeStruct((bsz, out_rows, DIFF_WIDTH), BF16),
        scratch_shapes=[pltpu.VMEM((2 * tq, seq), F32), pltpu.VMEM((2 * tq, seq), F32),
                        pltpu.VMEM((2 * tq, 1), F32), pltpu.VMEM((2 * tq, 1), F32)],
        compiler_params=_params("parallel", "parallel"),
        name="diff_attn",
    )(q, k, v, lam_p, nw)


def _merge_kernel(x_ref, mod_ref, ys_ref, yg_ref, yd_ref, wg_ref, bg_ref, ws_ref, wq_ref, wd_ref,
                  wo_ref, lg_ref, lb_ref, o_ref, *, n_ctx_rows):
    d = D_MODEL
    x = x_ref[...]
    h = _modulated(x_ref, mod_ref, 0, n_ctx_rows)
    m = None
    for i, (y_ref, w_ref) in enumerate(((ys_ref, ws_ref), (yg_ref, wq_ref), (yd_ref, wd_ref))):
        gate = _sigmoid(_dot(h, wg_ref[:, i * d:(i + 1) * d]) + bg_ref[:, i * d:(i + 1) * d])
        term = gate * _dot(y_ref[...], w_ref[...])
        m = term if m is None else m + term
    y = _dot(m.astype(BF16), wo_ref[...])
    r = DEEPNORM_ALPHA * x + _mod_row(mod_ref, 2, n_ctx_rows, x.shape[0]) * y
    o_ref[...] = _layer_norm(r, lg_ref[...], lb_ref[...])


def _merge_call(xs, mods, ys, yg, yd, lw, n_ctx_rows, tm, row_off):
    bsz, seq, d = xs.shape
    full = lambda w: pl.BlockSpec((None, tm, w), lambda b, t: (b, t + row_off, 0))
    tok = lambda w: pl.BlockSpec((None, tm, w), lambda b, t: (b, t, 0))
    consts = [lw["w_gate"], lw["b_gate"], lw["w_ssm_out"], lw["w_gqa_out"], lw["w_diff_out"],
              lw["w_o"], lw["ln1_g"], lw["ln1_b"]]
    return pl.pallas_call(
        functools.partial(_merge_kernel, n_ctx_rows=n_ctx_rows),
        grid=(bsz, seq // tm - row_off),
        in_specs=[full(d), _mod_spec(d),
                  full(SSM_D_INNER), tok(GQA_WIDTH), tok(DIFF_WIDTH)]
                 + [_const_spec(a.shape) for a in consts],
        out_specs=tok(d),
        out_shape=jax.ShapeDtypeStruct((bsz, seq - row_off * tm, d), F32),
        compiler_params=_params("parallel", "parallel"),
        name="merge",
    )(xs, mods, ys, yg, yd, *consts)


def _ffn_kernel(x_ref, mod_ref, wi_ref, wo_ref, lg_ref, lb_ref, o_ref, *, n_ctx_rows):
    x = x_ref[...]
    h = _modulated(x_ref, mod_ref, 3, n_ctx_rows)
    acc = None
    for c0 in range(0, FFN_HIDDEN, FFN_CHUNK):
        a = _dot(h, wi_ref[:, c0:c0 + FFN_CHUNK])
        b = _dot(h, wi_ref[:, FFN_HIDDEN + c0:FFN_HIDDEN + c0 + FFN_CHUNK])
        u = (_silu(a) * b).astype(BF16)
        part = _dot(u, wo_ref[c0:c0 + FFN_CHUNK, :])
        acc = part if acc is None else acc + part
    r = DEEPNORM_ALPHA * x + _mod_row(mod_ref, 5, n_ctx_rows, x.shape[0]) * acc
    o_ref[...] = _layer_norm(r, lg_ref[...], lb_ref[...])


def _ffn_call(xs, mods, lw, n_ctx_rows, tm):
    bsz, seq, d = xs.shape
    tok = pl.BlockSpec((None, tm, d), lambda b, t: (b, t, 0))
    consts = [lw["ffn_w_in"], lw["ffn_w_out"], lw["ln2_g"], lw["ln2_b"]]
    return pl.pallas_call(
        functools.partial(_ffn_kernel, n_ctx_rows=n_ctx_rows),
        grid=(bsz, seq // tm),
        in_specs=[tok, _mod_spec(d)]
                 + [_const_spec(a.shape) for a in consts],
        out_specs=tok,
        out_shape=jax.ShapeDtypeStruct((bsz, seq, d), F32),
        compiler_params=_params("parallel", "parallel"),
        name="ffn",
    )(xs, mods, *consts)


def _rope_tables(ctx_len, lat_len, head_dim, half_sign_period):
    t = jnp.arange(lat_len, dtype=jnp.int32)
    d_axis = head_dim // 2
    inv_freq = ROPE_THETA ** (-jnp.arange(0, d_axis, 2, dtype=F32) / d_axis)
    ang_r = (t // GRID_W).astype(F32)[:, None] * inv_freq
    ang_c = (t % GRID_W).astype(F32)[:, None] * inv_freq
    ang = jnp.concatenate([ang_r, ang_r, ang_c, ang_c], axis=-1)
    cos, sin = jnp.cos(ang), jnp.sin(ang)
    reps = LANES // head_dim
    cos, sin = jnp.tile(cos, (1, reps)), jnp.tile(sin, (1, reps))
    lane = jnp.arange(LANES)
    sign = jnp.where((lane % half_sign_period) < half_sign_period // 2, -1.0, 1.0).astype(F32)
    cos = jnp.concatenate([jnp.ones((ctx_len, LANES), F32), cos], axis=0)
    sin = jnp.concatenate([jnp.zeros((ctx_len, LANES), F32), sin * sign], axis=0)
    return cos, sin


def _group_rows(v, width):
    return v.reshape(SSM_GROUPS, 1, width)


def _ssm_params(conv_w, conv_b, dt_bias, a_log, d_skip, norm_w):
    gh, ng = SSM_GROUP_HEADS, SSM_GROUPS

    def taps(lo, width):
        w = conv_w[:, lo:lo + ng * width].reshape(SSM_CONV, ng, width)
        b = conv_b[lo:lo + ng * width].reshape(1, ng, width)
        packed = jnp.concatenate(
            [w, b, jnp.zeros((SUBLANES - SSM_CONV - 1, ng, width), F32)], axis=0)
        return jnp.transpose(packed, (1, 0, 2))

    def per_dir(v):
        r = jnp.transpose(v.reshape(2, ng, gh), (1, 0, 2)).reshape(ng, 1, 2 * gh)
        return jnp.pad(r, ((0, 0), (0, 0), (0, DT_WIDTH - 2 * gh)))

    return {
        "cwx": taps(0, SSM_GROUP_WIDTH),
        "cwb": taps(SSM_D_INNER, SSM_STATE),
        "cwc": taps(SSM_D_INNER + SSM_BC, SSM_STATE),
        "dtb": per_dir(dt_bias), "alog": per_dir(a_log),
        "dskip": _group_rows(jnp.repeat(d_skip, SSM_HEAD_DIM), SSM_GROUP_WIDTH),
        "nw": _group_rows(norm_w, SSM_GROUP_WIDTH),
    }


def kernel(x, c, ctx, c_ctx, ada_w, ada_b, w_in, b_gate, ssm_conv_w, ssm_conv_b, ssm_dt_bias,
           ssm_a_log, ssm_d, ssm_norm_w, w_ssm_out, gqa_q_norm, gqa_k_norm, w_gqa_out, diff_lambda,
           diff_norm_w, w_diff_out, w_o, ln1_g, ln1_b, ffn_w_in, ffn_w_out, ln2_g, ln2_b):
    bsz, lat_len, d = x.shape
    ctx_len = ctx.shape[1]
    depth = w_in.shape[0]
    assert d == D_MODEL and depth == DEPTH and w_in.shape[2] == IN_WIDTH
    assert lat_len % GRID_W == 0 and lat_len % SSM_CHUNK == 0 and ctx_len % SSM_CHUNK == 0
    tq = math.gcd(256, ctx_len)
    seq = ctx_len + lat_len
    tm = _row_tile(seq, tq)
    tm_lat = _row_tile(lat_len, tq)
    assert (lat_len // tq) % 2 == 0

    rows = -(-(bsz + 1) // SUBLANES) * SUBLANES
    cvec = jnp.concatenate([c, c_ctx[None], jnp.zeros((rows - bsz - 1, d), F32)], axis=0)
    ada = _ada_call(cvec, ada_w, ada_b)

    rope_g = _rope_tables(ctx_len, lat_len, GQA_HEAD_DIM, GQA_HEAD_DIM // 2)
    rope_d = _rope_tables(ctx_len, lat_len, DIFF_HEAD_DIM, DIFF_HEAD_DIM // 2)
    tabs = (*rope_g, *rope_d)

    xs = jnp.concatenate([ctx, x], axis=1)
    pad = jnp.zeros((d, DT_WIDTH - 2 * SSM_HEADS), BF16)
    for i in range(depth):
        need_ctx = i < depth - 1
        lam_init = 0.8 - 0.6 * math.exp(-0.3 * i)
        mod_l = ada[i, :bsz].reshape(bsz, 1, 6, d)
        mod_c = jnp.broadcast_to(ada[i, bsz].reshape(1, 1, 6, d), (bsz, 1, 6, d))
        mods = jnp.concatenate([mod_c, mod_l], axis=1)
        w = w_in[i].astype(BF16)
        w_ssm = jnp.concatenate([w[:, :OFF_GQ], pad], axis=1)
        w_attn = w[:, OFF_GQ:OFF_GATE]
        lw = {
            "w_gate": w[:, OFF_GATE:], "b_gate": b_gate[i][None],
            "w_ssm_out": w_ssm_out[i].astype(BF16), "w_gqa_out": w_gqa_out[i].astype(BF16),
            "w_diff_out": w_diff_out[i].astype(BF16), "w_o": w_o[i].astype(BF16),
            "ln1_g": ln1_g[i][None], "ln1_b": ln1_b[i][None],
            "ffn_w_in": ffn_w_in[i].astype(BF16), "ffn_w_out": ffn_w_out[i].astype(BF16),
            "ln2_g": ln2_g[i][None], "ln2_b": ln2_b[i][None],
        }
        sp = _ssm_params(ssm_conv_w[i], ssm_conv_b[i], ssm_dt_bias[i], ssm_a_log[i], ssm_d[i],
                         ssm_norm_w[i])

        z, xbc, dt = _inproj_ssm_call(xs, mods, w_ssm, ctx_len, tm)
        gq, gk, gvt, dq, dk, dv = _inproj_attn_call(
            xs, mods, w_attn, gqa_q_norm[i][None], gqa_k_norm[i][None], tabs, ctx_len, tm)
        y_ssm = _ssd_call(xbc, dt, z, sp, ctx_len)
        y_gqa = _gqa_call(gq, gk, gvt, ctx_len, tq, not need_ctx)
        y_diff = _diff_call(dq, dk, dv, diff_lambda[i], diff_norm_w[i][None], lam_init, ctx_len, tq,
                            not need_ctx)
        if need_ctx:
            x1 = _merge_call(xs, mods, y_ssm, y_gqa, y_diff, lw, ctx_len, tm, 0)
            xs = _ffn_call(x1, mods, lw, ctx_len, tm)
        else:
            x1 = _merge_call(xs, mods, y_ssm, y_gqa, y_diff, lw, 0, tq, ctx_len // tq)
            xs = _ffn_call(x1, mods, lw, 0, tm_lat)
    return xs
```

```python
import functools
import math

import jax
import jax.numpy as jnp
from jax import lax
from jax.experimental import pallas as pl
from jax.experimental.pallas import tpu as pltpu

F32 = jnp.float32
BF16 = jnp.bfloat16

D_MODEL = 1024
DEPTH = 4
GRID_W = 64
ROPE_THETA = 10000.0
EPS = 1e-6

SSM_D_INNER = 2048
SSM_HEAD_DIM = 64
SSM_HEADS = 32
SSM_GROUPS = 4
SSM_STATE = 128
SSM_CONV = 5
SSM_CHUNK = 128
SSM_BC = SSM_GROUPS * SSM_STATE
SSM_CONV_DIM = SSM_D_INNER + 2 * SSM_BC
SSM_GROUP_WIDTH = SSM_D_INNER // SSM_GROUPS
SSM_GROUP_HEADS = SSM_HEADS // SSM_GROUPS

GQA_HEAD_DIM = 128
GQA_HEADS = 8
GQA_KV_HEADS = 2
GQA_GROUP = GQA_HEADS // GQA_KV_HEADS
GQA_WIDTH = GQA_HEADS * GQA_HEAD_DIM
GQA_KV_WIDTH = GQA_KV_HEADS * GQA_HEAD_DIM

DIFF_HEAD_DIM = 64
DIFF_HEADS = 8
DIFF_WIDTH = DIFF_HEADS * 2 * DIFF_HEAD_DIM

N_BRANCHES = 3
FFN_HIDDEN = 2816
FFN_CHUNK = 1408

DEEPNORM_ALPHA = (2 * DEPTH) ** 0.25

LANES = 128
SUBLANES = 8
DT_WIDTH = LANES
SSM_CONV_TILES = SSM_CONV_DIM // LANES
LOG2E = 1.4426950408889634
VMEM_LIMIT = 56 * 1024 * 1024
MAX_ROW_TILE = 768
CONV_HALO = 16
SSM_ROW_TILE = 384

OFF_Z = 0
OFF_XBC = OFF_Z + SSM_D_INNER
OFF_DT = OFF_XBC + SSM_CONV_DIM
OFF_GQ = OFF_DT + 2 * SSM_HEADS
OFF_GK = OFF_GQ + GQA_WIDTH
OFF_GV = OFF_GK + GQA_KV_WIDTH
OFF_DQ = OFF_GV + GQA_KV_WIDTH
OFF_DK = OFF_DQ + DIFF_WIDTH
OFF_DV = OFF_DK + DIFF_WIDTH
OFF_GATE = OFF_DV + DIFF_WIDTH
IN_WIDTH = OFF_GATE + N_BRANCHES * D_MODEL


def _dot(a, b):
    return jnp.dot(a, b, preferred_element_type=F32)


def _dot_nt(a, b):
    return lax.dot_general(a, b, (((1,), (1,)), ((), ())), preferred_element_type=F32)


def _split3(a):
    hi = a.astype(BF16)
    r = a - hi.astype(F32)
    mid = r.astype(BF16)
    lo = (r - mid.astype(F32)).astype(BF16)
    return hi, mid, lo


def _dot_exact_rhs(a, sel):
    hi, mid, lo = _split3(a)
    return _dot(hi, sel) + _dot(mid, sel) + _dot(lo, sel)


def _sigmoid(x):
    return 1.0 / (1.0 + jnp.exp(-x))


def _silu(x):
    return x * _sigmoid(x)


def _layer_norm(r, g, b):
    mu = jnp.mean(r, axis=-1, keepdims=True)
    d = r - mu
    var = jnp.mean(d * d, axis=-1, keepdims=True)
    return d * lax.rsqrt(var + EPS) * g + b


def _params(*sem):
    return pltpu.CompilerParams(dimension_semantics=sem, vmem_limit_bytes=VMEM_LIMIT)


def _const_spec(shape):
    nd = len(shape)
    return pl.BlockSpec(shape, lambda *_: (0,) * nd, pipeline_mode=pl.Buffered(1))


def _ada_kernel(c_ref, w_ref, b_ref, o_ref):
    sc = _silu(c_ref[...]).astype(BF16)
    o_ref[...] = _dot(sc, w_ref[...].astype(BF16)) + b_ref[...]


def _ada_call(cvec, ada_w, ada_b):
    rows = cvec.shape[0]
    depth, d, n = ada_w.shape
    tn = n // 4
    return pl.pallas_call(
        _ada_kernel,
        grid=(depth, n // tn),
        in_specs=[
            pl.BlockSpec((rows, d), lambda i, j: (0, 0)),
            pl.BlockSpec((None, d, tn), lambda i, j: (i, 0, j)),
            pl.BlockSpec((None, 1, tn), lambda i, j: (i, 0, j)),
        ],
        out_specs=pl.BlockSpec((None, rows, tn), lambda i, j: (i, 0, j)),
        out_shape=jax.ShapeDtypeStruct((depth, rows, n), F32),
        compiler_params=_params("arbitrary", "arbitrary"),
        name="ada",
    )(cvec, ada_w, ada_b.reshape(depth, 1, n))


def _mod_row(mod_ref, row, n_ctx_rows, n, first=None):
    lat = mod_ref[1, row:row + 1, :]
    if n_ctx_rows == 0:
        return lat
    if first is None:
        first = pl.program_id(1) * n
    is_ctx = first + lax.broadcasted_iota(jnp.int32, (n, 1), 0) < n_ctx_rows
    return jnp.where(is_ctx, mod_ref[0, row:row + 1, :], lat)


def _modulated(x_ref, mod_ref, shift_row, n_ctx_rows, first=None):
    x = x_ref[...]
    shift = _mod_row(mod_ref, shift_row, n_ctx_rows, x.shape[0], first)
    scale = _mod_row(mod_ref, shift_row + 1, n_ctx_rows, x.shape[0], first)
    return (x * (1.0 + scale) + shift).astype(BF16)


def _inproj_ssm_kernel(x_ref, xp_ref, xn_ref, mod_ref, w_ref, cw_ref, z_ref, xc_ref, bc_ref, cc_ref,
                       dt_ref, u_scr, *, n_ctx_rows, seq):
    tm = x_ref.shape[0]
    halo = CONV_HALO
    first = pl.program_id(1) * tm
    h = _modulated(x_ref, mod_ref, 0, n_ctx_rows)
    h_all = jnp.concatenate([_modulated(xp_ref, mod_ref, 0, n_ctx_rows, first - halo), h,
                             _modulated(xn_ref, mod_ref, 0, n_ctx_rows, first + tm)], axis=0)
    step = 512
    for c0 in range(0, SSM_D_INNER, step):
        z_ref[:, c0:c0 + step] = _silu(_dot(h, w_ref[:, OFF_Z + c0:OFF_Z + c0 + step]))
    dt_ref[...] = _dot(h, w_ref[:, OFF_DT:OFF_DT + DT_WIDTH])

    prev_ok = jnp.logical_and(first != 0, first != n_ctx_rows)
    next_ok = jnp.logical_and(first + tm != seq, first + tm != n_ctx_rows)
    split = n_ctx_rows % tm
    center = SSM_CONV // 2

    def conv(j, lo, n, valid=None):
        cw = cw_ref[:, j * LANES:(j + 1) * LANES]
        acc = None
        for k in range(SSM_CONV):
            win = u_scr[j, halo + lo + k - center:halo + lo + k - center + n, :]
            if valid is not None and k != center:
                win = jnp.where(valid[k], win, 0.0)
            term = win * cw[k:k + 1, :]
            acc = term if acc is None else acc + term
        return _silu(acc + cw[SSM_CONV:SSM_CONV + 1, :])

    def dest(j):
        c = j * LANES
        if c < SSM_D_INNER:
            return xc_ref, slice(c, c + LANES)
        if c < SSM_D_INNER + SSM_BC:
            return bc_ref, slice(c - SSM_D_INNER, c - SSM_D_INNER + LANES)
        return cc_ref, slice(c - SSM_D_INNER - SSM_BC, c - SSM_D_INNER - SSM_BC + LANES)

    per = step // LANES
    for ci in range(SSM_CONV_DIM // step):
        c0 = OFF_XBC + ci * step
        u = _dot(h_all, w_ref[:, c0:c0 + step])
        for j in range(per):
            cols = slice(j * LANES, (j + 1) * LANES)
            t = ci * per + j
            u_scr[t, 0:halo, :] = jnp.where(prev_ok, u[0:halo, cols], 0.0)
            u_scr[t, halo:halo + tm, :] = u[halo:halo + tm, cols]
            u_scr[t, halo + tm:halo + tm + halo, :] = jnp.where(next_ok, u[halo + tm:, cols], 0.0)
        for j in range(per):
            ref, cols = dest(ci * per + j)
            ref[:, cols] = conv(ci * per + j, 0, tm)
    if split:
        @pl.when(pl.program_id(1) == n_ctx_rows // tm)
        def _():
            lo, n = split - SUBLANES, 2 * SUBLANES
            r = lo + lax.broadcasted_iota(jnp.int32, (n, LANES), 0)
            valid = [(r < split) == (r + k - center < split) for k in range(SSM_CONV)]
            for t in range(SSM_CONV_TILES):
                ref, cols = dest(t)
                ref[lo:lo + n, cols] = conv(t, lo, n, valid)


def _mod_spec(d):
    return pl.BlockSpec((None, 2, 6, d), lambda b, t: (b, 0, 0, 0))


def _row_tile(rows, unit, limit=None):
    best = unit
    for k in range(1, (limit or MAX_ROW_TILE) // unit + 1):
        if rows % (k * unit) == 0:
            best = k * unit
    return best


def _inproj_ssm_call(xs, mods, w_ssm, conv_wb, ctx_len, tm):
    bsz, seq, d = xs.shape
    halo = CONV_HALO
    per = tm // halo
    last = seq // halo - 1
    tok = lambda w: pl.BlockSpec((None, tm, w), lambda b, t: (b, t, 0))
    out = lambda w: jax.ShapeDtypeStruct((bsz, seq, w), F32)
    return pl.pallas_call(
        functools.partial(_inproj_ssm_kernel, n_ctx_rows=ctx_len, seq=seq),
        grid=(bsz, seq // tm),
        in_specs=[tok(d),
                  pl.BlockSpec((None, halo, d), lambda b, t: (b, jnp.maximum(t * per - 1, 0), 0)),
                  pl.BlockSpec((None, halo, d), lambda b, t: (b, jnp.minimum((t + 1) * per, last), 0)),
                  _mod_spec(d), _const_spec(w_ssm.shape), _const_spec(conv_wb.shape)],
        out_specs=[tok(SSM_D_INNER), tok(SSM_D_INNER), tok(SSM_BC), tok(SSM_BC), tok(DT_WIDTH)],
        out_shape=[out(SSM_D_INNER), out(SSM_D_INNER), out(SSM_BC), out(SSM_BC), out(DT_WIDTH)],
        scratch_shapes=[pltpu.VMEM((SSM_CONV_TILES, tm + 2 * halo, LANES), F32)],
        compiler_params=_params("parallel", "parallel"),
        name="inproj_ssm",
    )(xs, xs, xs, mods, w_ssm, conv_wb)


def _rope(u, cos, sin_signed, half):
    lane = lax.broadcasted_iota(jnp.int32, u.shape, 1)
    fwd = pltpu.roll(u, LANES - half, axis=1)
    bwd = pltpu.roll(u, half, axis=1)
    partner = jnp.where((lane % (2 * half)) < half, fwd, bwd)
    return u * cos + partner * sin_signed


def _rms_heads(u, g):
    ms = jnp.mean(u * u, axis=-1, keepdims=True)
    return u * lax.rsqrt(ms + EPS) * g


def _inproj_attn_kernel(x_ref, mod_ref, w_ref, qn_ref, kn_ref, cg_ref, sg_ref, cd_ref, sd_ref,
                        gq_ref, gk_ref, gv_ref, dq_ref, dk_ref, dv_ref, *, n_ctx_rows):
    h = _modulated(x_ref, mod_ref, 0, n_ctx_rows)
    base = OFF_GQ
    cg, sg = cg_ref[...], sg_ref[...]
    cd, sd = cd_ref[...], sd_ref[...]
    hd = GQA_HEAD_DIM

    def heads(off, width):
        step = min(512, width)
        for c0 in range(0, width, step):
            u = _dot(h, w_ref[:, off - base + c0:off - base + c0 + step])
            for j in range(step // LANES):
                yield (c0 // LANES + j) * LANES, u[:, j * LANES:(j + 1) * LANES]

    gqa_scale = GQA_HEAD_DIM ** -0.5 * LOG2E
    for c, u in heads(OFF_GQ, GQA_WIDTH):
        u = _rope(_rms_heads(u, qn_ref[...]), cg, sg, GQA_HEAD_DIM // 4)
        gq_ref[:, c:c + hd] = (u * gqa_scale).astype(BF16)
    for c, u in heads(OFF_GK, GQA_KV_WIDTH):
        u = _rope(_rms_heads(u, kn_ref[...]), cg, sg, GQA_HEAD_DIM // 4)
        gk_ref[:, c:c + hd] = u.astype(BF16)
    for c, u in heads(OFF_GV, GQA_KV_WIDTH):
        gv_ref[c:c + hd, :] = u.T.astype(BF16)
    scale = DIFF_HEAD_DIM ** -0.5 * LOG2E
    for c, u in heads(OFF_DQ, DIFF_WIDTH):
        dq_ref[:, c:c + LANES] = (_rope(u, cd, sd, DIFF_HEAD_DIM // 4) * scale).astype(BF16)
    for c, u in heads(OFF_DK, DIFF_WIDTH):
        dk_ref[:, c:c + LANES] = _rope(u, cd, sd, DIFF_HEAD_DIM // 4).astype(BF16)
    for c0 in range(0, DIFF_WIDTH, 512):
        dv_ref[:, c0:c0 + 512] = _dot(
            h, w_ref[:, OFF_DV - base + c0:OFF_DV - base + c0 + 512]).astype(BF16)


def _inproj_attn_call(xs, mods, w_attn, qn, kn, tabs, ctx_len, tm):
    bsz, seq, d = xs.shape
    tok = lambda w: pl.BlockSpec((None, tm, w), lambda b, t: (b, t, 0))
    tab = pl.BlockSpec((tm, LANES), lambda b, t: (t, 0))
    tr = lambda w: pl.BlockSpec((None, w, tm), lambda b, t: (b, 0, t))
    row = lambda w: jax.ShapeDtypeStruct((bsz, seq, w), BF16)
    col = lambda w: jax.ShapeDtypeStruct((bsz, w, seq), BF16)
    return pl.pallas_call(
        functools.partial(_inproj_attn_kernel, n_ctx_rows=ctx_len),
        grid=(bsz, seq // tm),
        in_specs=[tok(d), _mod_spec(d), _const_spec(w_attn.shape),
                  _const_spec(qn.shape), _const_spec(kn.shape), tab, tab, tab, tab],
        out_specs=[tok(GQA_WIDTH), tok(GQA_KV_WIDTH), tr(GQA_KV_WIDTH),
                   tok(DIFF_WIDTH), tok(DIFF_WIDTH), tok(DIFF_WIDTH)],
        out_shape=[row(GQA_WIDTH), row(GQA_KV_WIDTH), col(GQA_KV_WIDTH),
                   row(DIFF_WIDTH), row(DIFF_WIDTH), row(DIFF_WIDTH)],
        compiler_params=_params("parallel", "parallel"),
        name="inproj_attn",
    )(xs, mods, w_attn, qn, kn, *tabs)


def _ssd_kernel(xs_s, b_ref, c_s, dtr_ref, z_ref, dtb_ref, alog_ref, dskip_ref, nw_ref, o_ref,
                bt_s, cb_s, acs_s, acst_s, dtt_s, y_s, stf_s, stb_s, *, n_chunks, n_ctx_chunks):
    q = SSM_CHUNK
    gh = SSM_GROUP_HEADS
    hp = SSM_HEAD_DIM
    g = pl.program_id(1)

    row_i = lax.broadcasted_iota(jnp.int32, (q, q), 0)
    col_i = lax.broadcasted_iota(jnp.int32, (q, q), 1)
    tri_lo = (col_i <= row_i).astype(BF16)
    tri_up = (col_i >= row_i).astype(BF16)
    src = jnp.where(col_i < gh, g * gh + col_i, SSM_HEADS + g * gh + col_i - gh)
    pick = ((row_i == src) & (col_i < 2 * gh)).astype(BF16)
    a_neg = -jnp.exp(alog_ref[...])
    dt_bias = dtb_ref[...]

    for c in range(n_chunks):
        r0 = c * q
        bc = b_ref[r0:r0 + q, :]
        bt_s[c] = bc.T
        cb_s[c] = _dot_nt(c_s[r0:r0 + q, :].astype(BF16), bc.astype(BF16))
        raw = _dot_exact_rhs(dtr_ref[r0:r0 + q, :], pick) + dt_bias
        dtv = jnp.maximum(raw, 0.0) + jnp.log1p(jnp.exp(-jnp.abs(raw)))
        parts = _split3(dtv * a_neg)
        hi, mid, lo = parts
        acs = jnp.where(col_i < gh, _dot(tri_lo, hi) + _dot(tri_lo, mid) + _dot(tri_lo, lo),
                        _dot(tri_up, hi) + _dot(tri_up, mid) + _dot(tri_up, lo))
        acs_s[r0:r0 + q, :] = acs
        acst_s[c] = acs.T
        dtt_s[c] = dtv.T

    low_half = col_i < hp

    def scan_chunk(c, direction):
        st_s = (stf_s, stb_s)[direction]
        r0 = _aligned(c * q, q)
        rows = pl.ds(r0, q)
        xs = xs_s[rows, :]
        xb = xs.astype(BF16)
        cc = c_s[rows, :]
        acs = acs_s[rows, :]
        acst = acst_s[c]
        dtt = dtt_s[c]
        btf = bt_s[c]
        cb = cb_s[c]
        end = q - 1 if direction == 0 else 0
        tot = acst[:, end:end + 1]
        mine = (row_i >= gh * direction) & (row_i < gh * (direction + 1))
        w_out = dtt * jnp.exp(jnp.where(mine, tot - acst, 0.0))
        e_tot = jnp.exp(tot)
        keep = (row_i >= col_i) if direction == 0 else (col_i >= row_i)
        acs2 = acs * LOG2E
        src2 = (acst - jnp.log(dtt)) * LOG2E
        outs = []
        for pair in range(gh // 2):
            ms, ces, bws, ets = [], [], [], []
            for hh in range(2):
                col = gh * direction + 2 * pair + hh
                a_l = jnp.broadcast_to(acs2[:, col:col + 1], (q, q))
                decay = jnp.where(keep, jnp.exp2(a_l - src2[col:col + 1, :]), 0.0)
                ms.append((cb * decay).astype(BF16))
                ces.append((cc * jnp.exp2(a_l)).astype(BF16))
                bws.append((btf * w_out[col:col + 1, :]).astype(BF16))
                ets.append(e_tot[col:col + 1, :])
            slab = slice(pair * LANES, (pair + 1) * LANES)
            xp = xb[:, slab]
            zero = jnp.zeros_like(xp)
            x_bd = jnp.concatenate([jnp.where(low_half, xp, zero),
                                    jnp.where(low_half, zero, xp)], axis=0)
            st = st_s[:, slab]
            sb = st.astype(BF16)
            s_bd = jnp.concatenate([jnp.where(low_half, sb, zero),
                                    jnp.where(low_half, zero, sb)], axis=0)
            outs.append(_dot(jnp.concatenate(ms + ces, axis=1),
                             jnp.concatenate([x_bd, s_bd], axis=0)))
            grow = _dot(jnp.concatenate(bws, axis=1), x_bd)
            st_s[:, slab] = st * jnp.where(low_half, ets[0], ets[1]) + grow
        return rows, xs, jnp.concatenate(outs, axis=1)

    def finalize(rows, xs, y_scan):
        y = y_scan + dskip_ref[...] * xs
        y = y * z_ref[rows, :]
        ms = jnp.mean(y * y, axis=-1, keepdims=True)
        o_ref[rows, :] = (y * lax.rsqrt(ms + EPS) * nw_ref[...]).astype(BF16)

    stf_s[...] = jnp.zeros_like(stf_s)
    stb_s[...] = jnp.zeros_like(stb_s)

    def step(lo, hi, i, phase):
        rows_f, xs_f, yf = scan_chunk(lo + i, 0)
        if phase == "middle":
            _, _, yb = scan_chunk(lo + i, 1)
            finalize(rows_f, xs_f, yf + yb)
            return
        rows_b, xs_b, yb = scan_chunk(hi - 1 - i, 1)
        if phase == "park":
            y_s[rows_f, :] = yf
            y_s[rows_b, :] = yb
        else:
            finalize(rows_f, xs_f, yf + y_s[rows_f, :])
            finalize(rows_b, xs_b, y_s[rows_b, :] + yb)

    def segment(lo, hi):
        n = hi - lo
        half = n // 2
        if n <= 2:
            for i in range(n):
                step(lo, hi, i, "park" if i < half else ("middle" if 2 * i + 1 == n else "finish"))
            return

        def park(i, carry):
            step(lo, hi, i, "park")
            return carry

        def finish(i, carry):
            step(lo, hi, i, "finish")
            return carry

        lax.fori_loop(0, half, park, 0, unroll=2)
        if n % 2:
            step(lo, hi, half, "middle")
        lax.fori_loop(n - half, n, finish, 0, unroll=2)

    segment(0, n_ctx_chunks)
    segment(n_ctx_chunks, n_chunks)


def _ssd_call(xc, bc, cc, dt, z, sp, ctx_len):
    bsz, seq, _ = xc.shape
    q, gw, ng = SSM_CHUNK, SSM_GROUP_WIDTH, SSM_GROUPS
    n_chunks = seq // q
    grp = lambda r, w: pl.BlockSpec((None, r, w), lambda b, g: (g, 0, 0))
    col = lambda w: pl.BlockSpec((None, seq, w), lambda b, g: (b, 0, g))
    kern = functools.partial(_ssd_kernel, n_chunks=n_chunks, n_ctx_chunks=ctx_len // q)
    return pl.pallas_call(
        kern,
        grid=(bsz, ng),
        in_specs=[
            col(gw), col(SSM_STATE), col(SSM_STATE),
            pl.BlockSpec((None, seq, DT_WIDTH), lambda b, g: (b, 0, 0)),
            col(gw),
            grp(1, DT_WIDTH), grp(1, DT_WIDTH), grp(1, gw), grp(1, gw),
        ],
        out_specs=col(gw),
        out_shape=jax.ShapeDtypeStruct((bsz, seq, SSM_D_INNER), BF16),
        scratch_shapes=[
            pltpu.VMEM((n_chunks, SSM_STATE, q), F32),
            pltpu.VMEM((n_chunks, q, q), F32),
            pltpu.VMEM((seq, DT_WIDTH), F32),
            pltpu.VMEM((n_chunks, DT_WIDTH, q), F32),
            pltpu.VMEM((n_chunks, DT_WIDTH, q), F32),
            pltpu.VMEM((seq, gw), F32),
            pltpu.VMEM((SSM_STATE, gw), F32),
            pltpu.VMEM((SSM_STATE, gw), F32),
        ],
        compiler_params=_params("parallel", "arbitrary"),
        name="ssd",
    )(xc, bc, cc, dt, z, sp["dtb"], sp["alog"], sp["dskip"], sp["nw"])


def _aligned(start, multiple):
    return start if isinstance(start, int) else pl.multiple_of(start, multiple)


def _interleave(*stages):
    live = list(stages)
    while live:
        for st in list(live):
            if next(st, StopIteration) is StopIteration:
                live.remove(st)


def _gqa_kernel(q_ref, k_ref, vt_ref, o_ref, s_a, s_b, *, ctx_len, tq, skip_ctx):
    seq = k_ref.shape[0]
    kc = tq
    slots = (s_a, s_b)
    o_shift = ctx_len if skip_ctx else 0

    def scores(q_start, r, nk, out):
        q = q_ref[pl.ds(q_start, tq), r * GQA_HEAD_DIM:(r + 1) * GQA_HEAD_DIM]
        m = None
        for j in range(nk // kc):
            s = _dot_nt(k_ref[j * kc:(j + 1) * kc, :], q)
            slots[r % 2][j * kc:(j + 1) * kc, :] = s
            mj = jnp.max(s, axis=0, keepdims=True)
            m = mj if m is None else jnp.maximum(m, mj)
            yield
        out.append(m)

    def values(o_start, r, nk, m):
        l = None
        acc = None
        for j in range(nk // kc):
            p = jnp.exp2(slots[r % 2][j * kc:(j + 1) * kc, :] - m)
            lj = jnp.sum(p, axis=0, keepdims=True)
            l = lj if l is None else l + lj
            t = _dot(vt_ref[:, j * kc:(j + 1) * kc], p.astype(BF16))
            acc = t if acc is None else acc + t
            yield
        o_ref[pl.ds(o_start, tq), r * GQA_HEAD_DIM:(r + 1) * GQA_HEAD_DIM] = (
            (acc / l).T.astype(BF16))

    def tile(q_start, nk, m0, next_start):
        m = m0
        for r in range(GQA_GROUP):
            out = []
            stages = [values(_aligned(q_start - o_shift, tq), r, nk, m)]
            if r + 1 < GQA_GROUP:
                stages.append(scores(q_start, r + 1, nk, out))
            elif next_start is not None:
                stages.append(scores(next_start, 0, nk, out))
            _interleave(*stages)
            m = out[0] if out else None
        return m

    def first_scores(q_start, nk):
        out = []
        _interleave(scores(q_start, 0, nk, out))
        return out[0]

    if not skip_ctx:
        n_ctx = ctx_len // tq
        m = first_scores(0, ctx_len)
        for i in range(n_ctx):
            m = tile(i * tq, ctx_len, m, (i + 1) * tq if i + 1 < n_ctx else None)

    n_lat = (seq - ctx_len) // tq
    m = first_scores(ctx_len, seq)

    def body(i, m):
        start = pl.multiple_of(ctx_len + i * tq, tq)
        return tile(start, seq, m, pl.multiple_of(start + tq, tq))

    m = lax.fori_loop(0, n_lat - 1, body, m)
    tile(seq - tq, seq, m, None)


def _gqa_call(q, k, vt, ctx_len, tq, skip_ctx):
    bsz, seq, _ = q.shape
    gwid = GQA_GROUP * GQA_HEAD_DIM
    out_rows = seq - ctx_len if skip_ctx else seq
    kern = functools.partial(_gqa_kernel, ctx_len=ctx_len, tq=tq, skip_ctx=skip_ctx)
    return pl.pallas_call(
        kern,
        grid=(bsz, GQA_KV_HEADS),
        in_specs=[pl.BlockSpec((None, seq, gwid), lambda b, g: (b, 0, g)),
                  pl.BlockSpec((None, seq, GQA_HEAD_DIM), lambda b, g: (b, 0, g)),
                  pl.BlockSpec((None, GQA_HEAD_DIM, seq), lambda b, g: (b, g, 0))],
        out_specs=pl.BlockSpec((None, out_rows, gwid), lambda b, g: (b, 0, g)),
        out_shape=jax.ShapeDtypeStruct((bsz, out_rows, GQA_WIDTH), BF16),
        scratch_shapes=[pltpu.VMEM((seq, tq), F32), pltpu.VMEM((seq, tq), F32)],
        compiler_params=_params("parallel", "parallel"),
        name="gqa",
    )(q, k, vt)


def _diff_kernel(q_ref, k_ref, v_ref, lam_ref, nw_ref, o_ref, s_a, s_b, m_a, m_b, *, ctx_len, tq,
                 skip_ctx, lam_init):
    lv = lam_ref[...]
    lam = (jnp.exp(jnp.sum(lv[0:1] * lv[1:2], axis=-1, keepdims=True))
           - jnp.exp(jnp.sum(lv[2:3] * lv[3:4], axis=-1, keepdims=True)) + lam_init)
    seq = k_ref.shape[0]
    slots = (s_a, s_b)
    maxes = (m_a, m_b)
    o_shift = ctx_len if skip_ctx else 0

    def scores(q_start, nk, slot):
        q = q_ref[pl.ds(q_start, tq), :]
        lane = lax.broadcasted_iota(jnp.int32, q.shape, 1)
        zero = jnp.zeros_like(q)
        q2 = jnp.concatenate([jnp.where(lane < DIFF_HEAD_DIM, q, zero),
                              jnp.where(lane >= DIFF_HEAD_DIM, q, zero)], axis=0)
        s = _dot_nt(q2, k_ref[0:nk, :])
        slots[slot][:, 0:nk] = s
        maxes[slot][...] = jnp.max(s, axis=-1, keepdims=True)

    def finish(nk, slot, o_start):
        s = slots[slot][:, 0:nk]
        p = jnp.exp2(s - maxes[slot][...])
        l = jnp.sum(p, axis=-1, keepdims=True)
        l1, l2 = l[0:tq], l[tq:2 * tq]
        w = p[0:tq] - p[tq:2 * tq] * (lam * l1 / l2)
        o = _dot(w.astype(BF16), v_ref[0:nk, :]) * (1.0 / l1)
        ms = jnp.mean(o * o, axis=-1, keepdims=True)
        o_ref[pl.ds(o_start, tq), :] = (
            o * lax.rsqrt(ms + EPS) * nw_ref[...] * (1.0 - lam_init)).astype(BF16)

    if not skip_ctx:
        for i in range(ctx_len // tq):
            scores(i * tq, ctx_len, 0)
            finish(ctx_len, 0, i * tq)

    def pair(t0, last):
        scores(_aligned(t0 + tq, tq), seq, 1)
        finish(seq, 0, _aligned(t0 - o_shift, tq))
        if not last:
            scores(_aligned(t0 + 2 * tq, tq), seq, 0)
        finish(seq, 1, _aligned(t0 + tq - o_shift, tq))

    n_pairs = (seq - ctx_len) // (2 * tq)
    scores(ctx_len, seq, 0)
    for j in range(n_pairs):
        pair(ctx_len + 2 * j * tq, j == n_pairs - 1)


def _diff_call(q, k, v, lam_p, nw, lam_init, ctx_len, tq, skip_ctx):
    bsz, seq, _ = q.shape
    out_rows = seq - ctx_len if skip_ctx else seq
    kern = functools.partial(_diff_kernel, ctx_len=ctx_len, tq=tq, skip_ctx=skip_ctx,
                             lam_init=lam_init)
    head = pl.BlockSpec((None, seq, LANES), lambda b, h: (b, 0, h))
    return pl.pallas_call(
        kern,
        grid=(bsz, DIFF_HEADS),
        in_specs=[head, head, head, _const_spec(lam_p.shape), _const_spec(nw.shape)],
        out_specs=pl.BlockSpec((None, out_rows, LANES), lambda b, h: (b, 0, h)),
        out_shape=jax.ShapeDtypeStruct((bsz, out_rows, DIFF_WIDTH), BF16),
        scratch_shapes=[pltpu.VMEM((2 * tq, seq), F32), pltpu.VMEM((2 * tq, seq), F32),
                        pltpu.VMEM((2 * tq, 1), F32), pltpu.VMEM((2 * tq, 1), F32)],
        compiler_params=_params("parallel", "parallel"),
        name="diff_attn",
    )(q, k, v, lam_p, nw)


def _merge_kernel(x_ref, mod_ref, ys_ref, yg_ref, yd_ref, wg_ref, bg_ref, ws_ref, wq_ref, wd_ref,
                  wo_ref, lg_ref, lb_ref, o_ref, *, n_ctx_rows):
    d = D_MODEL
    x = x_ref[...]
    h = _modulated(x_ref, mod_ref, 0, n_ctx_rows)
    m = None
    for i, (y_ref, w_ref) in enumerate(((ys_ref, ws_ref), (yg_ref, wq_ref), (yd_ref, wd_ref))):
        gate = _sigmoid(_dot(h, wg_ref[:, i * d:(i + 1) * d]) + bg_ref[:, i * d:(i + 1) * d])
        term = gate * _dot(y_ref[...], w_ref[...])
        m = term if m is None else m + term
    y = _dot(m.astype(BF16), wo_ref[...])
    r = DEEPNORM_ALPHA * x + _mod_row(mod_ref, 2, n_ctx_rows, x.shape[0]) * y
    o_ref[...] = _layer_norm(r, lg_ref[...], lb_ref[...])


def _merge_call(xs, mods, ys, yg, yd, lw, n_ctx_rows, tm, row_off):
    bsz, seq, d = xs.shape
    full = lambda w: pl.BlockSpec((None, tm, w), lambda b, t: (b, t + row_off, 0))
    tok = lambda w: pl.BlockSpec((None, tm, w), lambda b, t: (b, t, 0))
    consts = [lw["w_gate"], lw["b_gate"], lw["w_ssm_out"], lw["w_gqa_out"], lw["w_diff_out"],
              lw["w_o"], lw["ln1_g"], lw["ln1_b"]]
    return pl.pallas_call(
        functools.partial(_merge_kernel, n_ctx_rows=n_ctx_rows),
        grid=(bsz, seq // tm - row_off),
        in_specs=[full(d), _mod_spec(d),
                  full(SSM_D_INNER), tok(GQA_WIDTH), tok(DIFF_WIDTH)]
                 + [_const_spec(a.shape) for a in consts],
        out_specs=tok(d),
        out_shape=jax.ShapeDtypeStruct((bsz, seq - row_off * tm, d), F32),
        compiler_params=_params("parallel", "parallel"),
        name="merge",
    )(xs, mods, ys, yg, yd, *consts)


def _ffn_kernel(x_ref, mod_ref, wi_ref, wo_ref, lg_ref, lb_ref, o_ref, *, n_ctx_rows):
    x = x_ref[...]
    h = _modulated(x_ref, mod_ref, 3, n_ctx_rows)
    acc = None
    for c0 in range(0, FFN_HIDDEN, FFN_CHUNK):
        a = _dot(h, wi_ref[:, c0:c0 + FFN_CHUNK])
        b = _dot(h, wi_ref[:, FFN_HIDDEN + c0:FFN_HIDDEN + c0 + FFN_CHUNK])
        u = (_silu(a) * b).astype(BF16)
        part = _dot(u, wo_ref[c0:c0 + FFN_CHUNK, :])
        acc = part if acc is None else acc + part
    r = DEEPNORM_ALPHA * x + _mod_row(mod_ref, 5, n_ctx_rows, x.shape[0]) * acc
    o_ref[...] = _layer_norm(r, lg_ref[...], lb_ref[...])


def _ffn_call(xs, mods, lw, n_ctx_rows, tm):
    bsz, seq, d = xs.shape
    tok = pl.BlockSpec((None, tm, d), lambda b, t: (b, t, 0))
    consts = [lw["ffn_w_in"], lw["ffn_w_out"], lw["ln2_g"], lw["ln2_b"]]
    return pl.pallas_call(
        functools.partial(_ffn_kernel, n_ctx_rows=n_ctx_rows),
        grid=(bsz, seq // tm),
        in_specs=[tok, _mod_spec(d)]
                 + [_const_spec(a.shape) for a in consts],
        out_specs=tok,
        out_shape=jax.ShapeDtypeStruct((bsz, seq, d), F32),
        compiler_params=_params("parallel", "parallel"),
        name="ffn",
    )(xs, mods, *consts)


def _rope_tables(ctx_len, lat_len, head_dim, half_sign_period):
    t = jnp.arange(lat_len, dtype=jnp.int32)
    d_axis = head_dim // 2
    inv_freq = ROPE_THETA ** (-jnp.arange(0, d_axis, 2, dtype=F32) / d_axis)
    ang_r = (t // GRID_W).astype(F32)[:, None] * inv_freq
    ang_c = (t % GRID_W).astype(F32)[:, None] * inv_freq
    ang = jnp.concatenate([ang_r, ang_r, ang_c, ang_c], axis=-1)
    cos, sin = jnp.cos(ang), jnp.sin(ang)
    reps = LANES // head_dim
    cos, sin = jnp.tile(cos, (1, reps)), jnp.tile(sin, (1, reps))
    lane = jnp.arange(LANES)
    sign = jnp.where((lane % half_sign_period) < half_sign_period // 2, -1.0, 1.0).astype(F32)
    cos = jnp.concatenate([jnp.ones((ctx_len, LANES), F32), cos], axis=0)
    sin = jnp.concatenate([jnp.zeros((ctx_len, LANES), F32), sin * sign], axis=0)
    return cos, sin


def _group_rows(v, width):
    return v.reshape(SSM_GROUPS, 1, width)


def _ssm_params(dt_bias, a_log, d_skip, norm_w):
    gh, ng = SSM_GROUP_HEADS, SSM_GROUPS

    def per_dir(v):
        r = jnp.transpose(v.reshape(2, ng, gh), (1, 0, 2)).reshape(ng, 1, 2 * gh)
        return jnp.pad(r, ((0, 0), (0, 0), (0, DT_WIDTH - 2 * gh)))

    return {
        "dtb": per_dir(dt_bias), "alog": per_dir(a_log),
        "dskip": _group_rows(jnp.repeat(d_skip, SSM_HEAD_DIM), SSM_GROUP_WIDTH),
        "nw": _group_rows(norm_w, SSM_GROUP_WIDTH),
    }


def kernel(x, c, ctx, c_ctx, ada_w, ada_b, w_in, b_gate, ssm_conv_w, ssm_conv_b, ssm_dt_bias,
           ssm_a_log, ssm_d, ssm_norm_w, w_ssm_out, gqa_q_norm, gqa_k_norm, w_gqa_out, diff_lambda,
           diff_norm_w, w_diff_out, w_o, ln1_g, ln1_b, ffn_w_in, ffn_w_out, ln2_g, ln2_b):
    bsz, lat_len, d = x.shape
    ctx_len = ctx.shape[1]
    depth = w_in.shape[0]
    assert d == D_MODEL and depth == DEPTH and w_in.shape[2] == IN_WIDTH
    assert lat_len % GRID_W == 0 and lat_len % SSM_CHUNK == 0 and ctx_len % SSM_CHUNK == 0
    tq = math.gcd(256, ctx_len)
    seq = ctx_len + lat_len
    tm = _row_tile(seq, tq)
    tm_lat = _row_tile(lat_len, tq)
    tm_ssm = _row_tile(seq, SSM_CHUNK, SSM_ROW_TILE)
    assert (lat_len // tq) % 2 == 0

    rows = -(-(bsz + 1) // SUBLANES) * SUBLANES
    cvec = jnp.concatenate([c, c_ctx[None], jnp.zeros((rows - bsz - 1, d), F32)], axis=0)
    ada = _ada_call(cvec, ada_w, ada_b)

    rope_g = _rope_tables(ctx_len, lat_len, GQA_HEAD_DIM, GQA_HEAD_DIM // 2)
    rope_d = _rope_tables(ctx_len, lat_len, DIFF_HEAD_DIM, DIFF_HEAD_DIM // 2)
    tabs = (*rope_g, *rope_d)

    xs = jnp.concatenate([ctx, x], axis=1)
    pad = jnp.zeros((d, DT_WIDTH - 2 * SSM_HEADS), BF16)
    for i in range(depth):
        need_ctx = i < depth - 1
        lam_init = 0.8 - 0.6 * math.exp(-0.3 * i)
        mod_l = ada[i, :bsz].reshape(bsz, 1, 6, d)
        mod_c = jnp.broadcast_to(ada[i, bsz].reshape(1, 1, 6, d), (bsz, 1, 6, d))
        mods = jnp.concatenate([mod_c, mod_l], axis=1)
        w = w_in[i].astype(BF16)
        w_ssm = jnp.concatenate([w[:, :OFF_GQ], pad], axis=1)
        w_attn = w[:, OFF_GQ:OFF_GATE]
        lw = {
            "w_gate": w[:, OFF_GATE:], "b_gate": b_gate[i][None],
            "w_ssm_out": w_ssm_out[i].astype(BF16), "w_gqa_out": w_gqa_out[i].astype(BF16),
            "w_diff_out": w_diff_out[i].astype(BF16), "w_o": w_o[i].astype(BF16),
            "ln1_g": ln1_g[i][None], "ln1_b": ln1_b[i][None],
            "ffn_w_in": ffn_w_in[i].astype(BF16), "ffn_w_out": ffn_w_out[i].astype(BF16),
            "ln2_g": ln2_g[i][None], "ln2_b": ln2_b[i][None],
        }
        sp = _ssm_params(ssm_dt_bias[i], ssm_a_log[i], ssm_d[i], ssm_norm_w[i])
        conv_wb = jnp.concatenate([ssm_conv_w[i], ssm_conv_b[i][None],
                                   jnp.zeros((SUBLANES - SSM_CONV - 1, SSM_CONV_DIM), F32)], axis=0)

        z, xc, bc, cc, dt = _inproj_ssm_call(xs, mods, w_ssm, conv_wb, ctx_len, tm_ssm)
        gq, gk, gvt, dq, dk, dv = _inproj_attn_call(
            xs, mods, w_attn, gqa_q_norm[i][None], gqa_k_norm[i][None], tabs, ctx_len, tm)
        y_ssm = _ssd_call(xc, bc, cc, dt, z, sp, ctx_len)
        y_gqa = _gqa_call(gq, gk, gvt, ctx_len, tq, not need_ctx)
        y_diff = _diff_call(dq, dk, dv, diff_lambda[i], diff_norm_w[i][None], lam_init, ctx_len, tq,
                            not need_ctx)
        if need_ctx:
            x1 = _merge_call(xs, mods, y_ssm, y_gqa, y_diff, lw, ctx_len, tm, 0)
            xs = _ffn_call(x1, mods, lw, ctx_len, tm)
        else:
            x1 = _merge_call(xs, mods, y_ssm, y_gqa, y_diff, lw, 0, tq, ctx_len // tq)
            xs = _ffn_call(x1, mods, lw, 0, tm_lat)
    return xs
```

```python
import functools
import math

import jax
import jax.numpy as jnp
from jax import lax
from jax.experimental import pallas as pl
from jax.experimental.pallas import tpu as pltpu

F32 = jnp.float32
BF16 = jnp.bfloat16

D_MODEL = 1024
DEPTH = 4
GRID_W = 64
ROPE_THETA = 10000.0
EPS = 1e-6

SSM_D_INNER = 2048
SSM_HEAD_DIM = 64
SSM_HEADS = 32
SSM_GROUPS = 4
SSM_STATE = 128
SSM_CONV = 5
SSM_CHUNK = 128
SSM_BC = SSM_GROUPS * SSM_STATE
SSM_CONV_DIM = SSM_D_INNER + 2 * SSM_BC
SSM_GROUP_WIDTH = SSM_D_INNER // SSM_GROUPS
SSM_GROUP_HEADS = SSM_HEADS // SSM_GROUPS

GQA_HEAD_DIM = 128
GQA_HEADS = 8
GQA_KV_HEADS = 2
GQA_GROUP = GQA_HEADS // GQA_KV_HEADS
GQA_WIDTH = GQA_HEADS * GQA_HEAD_DIM
GQA_KV_WIDTH = GQA_KV_HEADS * GQA_HEAD_DIM

DIFF_HEAD_DIM = 64
DIFF_HEADS = 8
DIFF_WIDTH = DIFF_HEADS * 2 * DIFF_HEAD_DIM

N_BRANCHES = 3
FFN_HIDDEN = 2816
FFN_CHUNK = 256

DEEPNORM_ALPHA = (2 * DEPTH) ** 0.25

LANES = 128
SUBLANES = 8
DT_WIDTH = LANES
SSM_CONV_TILES = SSM_CONV_DIM // LANES
LOG2E = 1.4426950408889634
VMEM_LIMIT = 56 * 1024 * 1024
MAX_ROW_TILE = 768
CONV_HALO = 16
SSM_ROW_TILE = 384

OFF_Z = 0
OFF_XBC = OFF_Z + SSM_D_INNER
OFF_DT = OFF_XBC + SSM_CONV_DIM
OFF_GQ = OFF_DT + 2 * SSM_HEADS
OFF_GK = OFF_GQ + GQA_WIDTH
OFF_GV = OFF_GK + GQA_KV_WIDTH
OFF_DQ = OFF_GV + GQA_KV_WIDTH
OFF_DK = OFF_DQ + DIFF_WIDTH
OFF_DV = OFF_DK + DIFF_WIDTH
OFF_GATE = OFF_DV + DIFF_WIDTH
IN_WIDTH = OFF_GATE + N_BRANCHES * D_MODEL


def _dot(a, b):
    return jnp.dot(a, b, preferred_element_type=F32)


def _dot_nt(a, b):
    return lax.dot_general(a, b, (((1,), (1,)), ((), ())), preferred_element_type=F32)


def _split3(a):
    hi = a.astype(BF16)
    r = a - hi.astype(F32)
    mid = r.astype(BF16)
    lo = (r - mid.astype(F32)).astype(BF16)
    return hi, mid, lo


def _dot_exact_rhs(a, sel):
    hi, mid, lo = _split3(a)
    return _dot(hi, sel) + _dot(mid, sel) + _dot(lo, sel)


def _sigmoid(x):
    return 1.0 / (1.0 + jnp.exp(-x))


def _silu(x):
    return x * _sigmoid(x)


def _layer_norm(r, g, b):
    mu = jnp.mean(r, axis=-1, keepdims=True)
    d = r - mu
    var = jnp.mean(d * d, axis=-1, keepdims=True)
    return d * lax.rsqrt(var + EPS) * g + b


def _params(*sem):
    return pltpu.CompilerParams(dimension_semantics=sem, vmem_limit_bytes=VMEM_LIMIT)


def _const_spec(shape):
    nd = len(shape)
    return pl.BlockSpec(shape, lambda *_: (0,) * nd, pipeline_mode=pl.Buffered(1))


def _ada_kernel(c_ref, w_ref, b_ref, o_ref):
    sc = _silu(c_ref[...]).astype(BF16)
    o_ref[...] = _dot(sc, w_ref[...].astype(BF16)) + b_ref[...]


def _ada_call(cvec, ada_w, ada_b):
    rows = cvec.shape[0]
    depth, d, n = ada_w.shape
    tn = n // 4
    return pl.pallas_call(
        _ada_kernel,
        grid=(depth, n // tn),
        in_specs=[
            pl.BlockSpec((rows, d), lambda i, j: (0, 0)),
            pl.BlockSpec((None, d, tn), lambda i, j: (i, 0, j)),
            pl.BlockSpec((None, 1, tn), lambda i, j: (i, 0, j)),
        ],
        out_specs=pl.BlockSpec((None, rows, tn), lambda i, j: (i, 0, j)),
        out_shape=jax.ShapeDtypeStruct((depth, rows, n), F32),
        compiler_params=_params("arbitrary", "arbitrary"),
        name="ada",
    )(cvec, ada_w, ada_b.reshape(depth, 1, n))


def _mod_row(mod_ref, row, n_ctx_rows, n, first=None):
    lat = mod_ref[1, row:row + 1, :]
    if n_ctx_rows == 0:
        return lat
    if first is None:
        first = pl.program_id(1) * n
    is_ctx = first + lax.broadcasted_iota(jnp.int32, (n, 1), 0) < n_ctx_rows
    return jnp.where(is_ctx, mod_ref[0, row:row + 1, :], lat)


def _modulated(x_ref, mod_ref, shift_row, n_ctx_rows, first=None):
    x = x_ref[...]
    shift = _mod_row(mod_ref, shift_row, n_ctx_rows, x.shape[0], first)
    scale = _mod_row(mod_ref, shift_row + 1, n_ctx_rows, x.shape[0], first)
    return (x * (1.0 + scale) + shift).astype(BF16)


def _inproj_ssm_kernel(x_ref, xp_ref, xn_ref, mod_ref, w_ref, cw_ref, z_ref, xc_ref, bc_ref, cc_ref,
                       dt_ref, u_scr, *, n_ctx_rows, seq):
    tm = x_ref.shape[0]
    halo = CONV_HALO
    first = pl.program_id(1) * tm
    h = _modulated(x_ref, mod_ref, 0, n_ctx_rows)
    h_all = jnp.concatenate([_modulated(xp_ref, mod_ref, 0, n_ctx_rows, first - halo), h,
                             _modulated(xn_ref, mod_ref, 0, n_ctx_rows, first + tm)], axis=0)
    step = 512
    for c0 in range(0, SSM_D_INNER, step):
        z_ref[:, c0:c0 + step] = _silu(_dot(h, w_ref[:, OFF_Z + c0:OFF_Z + c0 + step]))
    dt_ref[...] = _dot(h, w_ref[:, OFF_DT:OFF_DT + DT_WIDTH])

    prev_ok = jnp.logical_and(first != 0, first != n_ctx_rows)
    next_ok = jnp.logical_and(first + tm != seq, first + tm != n_ctx_rows)
    split = n_ctx_rows % tm
    center = SSM_CONV // 2

    def conv(j, lo, n, valid=None):
        cw = cw_ref[:, j * LANES:(j + 1) * LANES]
        acc = None
        for k in range(SSM_CONV):
            win = u_scr[j, halo + lo + k - center:halo + lo + k - center + n, :]
            if valid is not None and k != center:
                win = jnp.where(valid[k], win, 0.0)
            term = win * cw[k:k + 1, :]
            acc = term if acc is None else acc + term
        return _silu(acc + cw[SSM_CONV:SSM_CONV + 1, :])

    def dest(j):
        c = j * LANES
        if c < SSM_D_INNER:
            return xc_ref, slice(c, c + LANES)
        if c < SSM_D_INNER + SSM_BC:
            return bc_ref, slice(c - SSM_D_INNER, c - SSM_D_INNER + LANES)
        return cc_ref, slice(c - SSM_D_INNER - SSM_BC, c - SSM_D_INNER - SSM_BC + LANES)

    per = step // LANES
    for ci in range(SSM_CONV_DIM // step):
        c0 = OFF_XBC + ci * step
        u = _dot(h_all, w_ref[:, c0:c0 + step])
        for j in range(per):
            cols = slice(j * LANES, (j + 1) * LANES)
            t = ci * per + j
            u_scr[t, 0:halo, :] = jnp.where(prev_ok, u[0:halo, cols], 0.0)
            u_scr[t, halo:halo + tm, :] = u[halo:halo + tm, cols]
            u_scr[t, halo + tm:halo + tm + halo, :] = jnp.where(next_ok, u[halo + tm:, cols], 0.0)
        for j in range(per):
            ref, cols = dest(ci * per + j)
            ref[:, cols] = conv(ci * per + j, 0, tm)
    if split:
        @pl.when(pl.program_id(1) == n_ctx_rows // tm)
        def _():
            lo, n = split - SUBLANES, 2 * SUBLANES
            r = lo + lax.broadcasted_iota(jnp.int32, (n, LANES), 0)
            valid = [(r < split) == (r + k - center < split) for k in range(SSM_CONV)]
            for t in range(SSM_CONV_TILES):
                ref, cols = dest(t)
                ref[lo:lo + n, cols] = conv(t, lo, n, valid)


def _mod_spec(d):
    return pl.BlockSpec((None, 2, 6, d), lambda b, t: (b, 0, 0, 0))


def _row_tile(rows, unit, limit=None):
    best = unit
    for k in range(1, (limit or MAX_ROW_TILE) // unit + 1):
        if rows % (k * unit) == 0:
            best = k * unit
    return best


def _inproj_ssm_call(xs, mods, w_ssm, conv_wb, ctx_len, tm):
    bsz, seq, d = xs.shape
    halo = CONV_HALO
    per = tm // halo
    last = seq // halo - 1
    tok = lambda w: pl.BlockSpec((None, tm, w), lambda b, t: (b, t, 0))
    out = lambda w: jax.ShapeDtypeStruct((bsz, seq, w), F32)
    return pl.pallas_call(
        functools.partial(_inproj_ssm_kernel, n_ctx_rows=ctx_len, seq=seq),
        grid=(bsz, seq // tm),
        in_specs=[tok(d),
                  pl.BlockSpec((None, halo, d), lambda b, t: (b, jnp.maximum(t * per - 1, 0), 0)),
                  pl.BlockSpec((None, halo, d), lambda b, t: (b, jnp.minimum((t + 1) * per, last), 0)),
                  _mod_spec(d), _const_spec(w_ssm.shape), _const_spec(conv_wb.shape)],
        out_specs=[tok(SSM_D_INNER), tok(SSM_D_INNER), tok(SSM_BC), tok(SSM_BC), tok(DT_WIDTH)],
        out_shape=[out(SSM_D_INNER), out(SSM_D_INNER), out(SSM_BC), out(SSM_BC), out(DT_WIDTH)],
        scratch_shapes=[pltpu.VMEM((SSM_CONV_TILES, tm + 2 * halo, LANES), F32)],
        compiler_params=_params("parallel", "parallel"),
        name="inproj_ssm",
    )(xs, xs, xs, mods, w_ssm, conv_wb)


def _rope(u, cos, sin_signed, half):
    lane = lax.broadcasted_iota(jnp.int32, u.shape, 1)
    fwd = pltpu.roll(u, LANES - half, axis=1)
    bwd = pltpu.roll(u, half, axis=1)
    partner = jnp.where((lane % (2 * half)) < half, fwd, bwd)
    return u * cos + partner * sin_signed


def _rms_heads(u, g):
    ms = jnp.mean(u * u, axis=-1, keepdims=True)
    return u * lax.rsqrt(ms + EPS) * g


def _inproj_attn_kernel(x_ref, mod_ref, w_ref, qn_ref, kn_ref, cg_ref, sg_ref, cd_ref, sd_ref,
                        gq_ref, gk_ref, gv_ref, dq_ref, dk_ref, dv_ref, *, n_ctx_rows):
    h = _modulated(x_ref, mod_ref, 0, n_ctx_rows)
    base = OFF_GQ
    cg, sg = cg_ref[...], sg_ref[...]
    cd, sd = cd_ref[...], sd_ref[...]
    hd = GQA_HEAD_DIM

    def heads(off, width):
        step = min(512, width)
        for c0 in range(0, width, step):
            u = _dot(h, w_ref[:, off - base + c0:off - base + c0 + step])
            for j in range(step // LANES):
                yield (c0 // LANES + j) * LANES, u[:, j * LANES:(j + 1) * LANES]

    gqa_scale = GQA_HEAD_DIM ** -0.5 * LOG2E
    for c, u in heads(OFF_GQ, GQA_WIDTH):
        u = _rope(_rms_heads(u, qn_ref[...]), cg, sg, GQA_HEAD_DIM // 4)
        gq_ref[:, c:c + hd] = (u * gqa_scale).astype(BF16)
    for c, u in heads(OFF_GK, GQA_KV_WIDTH):
        u = _rope(_rms_heads(u, kn_ref[...]), cg, sg, GQA_HEAD_DIM // 4)
        gk_ref[:, c:c + hd] = u.astype(BF16)
    for c, u in heads(OFF_GV, GQA_KV_WIDTH):
        gv_ref[c:c + hd, :] = u.T.astype(BF16)
    scale = DIFF_HEAD_DIM ** -0.5 * LOG2E
    for c, u in heads(OFF_DQ, DIFF_WIDTH):
        dq_ref[:, c:c + LANES] = (_rope(u, cd, sd, DIFF_HEAD_DIM // 4) * scale).astype(BF16)
    for c, u in heads(OFF_DK, DIFF_WIDTH):
        dk_ref[:, c:c + LANES] = _rope(u, cd, sd, DIFF_HEAD_DIM // 4).astype(BF16)
    for c0 in range(0, DIFF_WIDTH, 512):
        dv_ref[:, c0:c0 + 512] = _dot(
            h, w_ref[:, OFF_DV - base + c0:OFF_DV - base + c0 + 512]).astype(BF16)


def _inproj_attn_call(xs, mods, w_attn, qn, kn, tabs, ctx_len, tm):
    bsz, seq, d = xs.shape
    tok = lambda w: pl.BlockSpec((None, tm, w), lambda b, t: (b, t, 0))
    tab = pl.BlockSpec((tm, LANES), lambda b, t: (t, 0))
    tr = lambda w: pl.BlockSpec((None, w, tm), lambda b, t: (b, 0, t))
    row = lambda w: jax.ShapeDtypeStruct((bsz, seq, w), BF16)
    col = lambda w: jax.ShapeDtypeStruct((bsz, w, seq), BF16)
    return pl.pallas_call(
        functools.partial(_inproj_attn_kernel, n_ctx_rows=ctx_len),
        grid=(bsz, seq // tm),
        in_specs=[tok(d), _mod_spec(d), _const_spec(w_attn.shape),
                  _const_spec(qn.shape), _const_spec(kn.shape), tab, tab, tab, tab],
        out_specs=[tok(GQA_WIDTH), tok(GQA_KV_WIDTH), tr(GQA_KV_WIDTH),
                   tok(DIFF_WIDTH), tok(DIFF_WIDTH), tok(DIFF_WIDTH)],
        out_shape=[row(GQA_WIDTH), row(GQA_KV_WIDTH), col(GQA_KV_WIDTH),
                   row(DIFF_WIDTH), row(DIFF_WIDTH), row(DIFF_WIDTH)],
        compiler_params=_params("parallel", "parallel"),
        name="inproj_attn",
    )(xs, mods, w_attn, qn, kn, *tabs)


def _ssd_kernel(xs_s, b_ref, c_s, dtr_ref, z_ref, dtb_ref, alog_ref, dskip_ref, nw_ref, o_ref,
                bt_s, cb_s, acs_s, acst_s, dtt_s, y_s, stf_s, stb_s, *, n_chunks, n_ctx_chunks):
    q = SSM_CHUNK
    gh = SSM_GROUP_HEADS
    hp = SSM_HEAD_DIM
    g = pl.program_id(1)

    row_i = lax.broadcasted_iota(jnp.int32, (q, q), 0)
    col_i = lax.broadcasted_iota(jnp.int32, (q, q), 1)
    tri_lo = (col_i <= row_i).astype(BF16)
    tri_up = (col_i >= row_i).astype(BF16)
    src = jnp.where(col_i < gh, g * gh + col_i, SSM_HEADS + g * gh + col_i - gh)
    pick = ((row_i == src) & (col_i < 2 * gh)).astype(BF16)
    a_neg = -jnp.exp(alog_ref[...])
    dt_bias = dtb_ref[...]

    seq = n_chunks * q
    raw = _dot_exact_rhs(dtr_ref[...], pick) + dt_bias
    dtv = jnp.maximum(raw, 0.0) + jnp.log1p(jnp.exp(-jnp.abs(raw)))
    a_cols = jnp.broadcast_to(a_neg, (q, q)).T
    for c in range(n_chunks):
        rows = slice(c * q, (c + 1) * q)
        dt_t = dtv[rows, :].T
        dtt_s[rows, :] = dt_t
        acst_s[rows, :] = dt_t * a_cols
        bc = b_ref[rows, :]
        bt_s[rows, :] = bc.T
        cb_s[rows, :] = _dot_nt(c_s[rows, :].astype(BF16), bc.astype(BF16))
    hi, mid, lo = _split3(acst_s[...])
    prefix = _dot(hi, tri_up) + _dot(mid, tri_up) + _dot(lo, tri_up)
    suffix = _dot(hi, tri_lo) + _dot(mid, tri_lo) + _dot(lo, tri_lo)
    head_col = lax.broadcasted_iota(jnp.int32, (seq, q), 0) & (q - 1)
    acst_all = jnp.where(head_col < gh, prefix, suffix)
    acst_s[...] = acst_all
    for c in range(n_chunks):
        rows = slice(c * q, (c + 1) * q)
        acs_s[rows, :] = acst_all[rows, :].T

    low_half = col_i < hp

    def scan_chunk(c, direction):
        st_s = (stf_s, stb_s)[direction]
        r0 = _aligned(c * q, q)
        rows = pl.ds(r0, q)
        xs = xs_s[rows, :]
        xb = xs.astype(BF16)
        cc = c_s[rows, :]
        acs = acs_s[rows, :]
        acst = acst_s[rows, :]
        dtt = dtt_s[rows, :]
        btf = bt_s[rows, :]
        cb = cb_s[rows, :]
        end = q - 1 if direction == 0 else 0
        tot = acst[:, end:end + 1]
        mine = (row_i >= gh * direction) & (row_i < gh * (direction + 1))
        w_out = dtt * jnp.exp(jnp.where(mine, tot - acst, 0.0))
        e_tot = jnp.exp(tot)
        keep = (row_i >= col_i) if direction == 0 else (col_i >= row_i)
        acs2 = acs * LOG2E
        src2 = (acst - jnp.log(dtt)) * LOG2E
        outs = []
        for pair in range(gh // 2):
            ms, ces, bws, ets = [], [], [], []
            for hh in range(2):
                col = gh * direction + 2 * pair + hh
                a_l = jnp.broadcast_to(acs2[:, col:col + 1], (q, q))
                decay = jnp.where(keep, jnp.exp2(a_l - src2[col:col + 1, :]), 0.0)
                ms.append((cb * decay).astype(BF16))
                ces.append((cc * jnp.exp2(a_l)).astype(BF16))
                bws.append((btf * w_out[col:col + 1, :]).astype(BF16))
                ets.append(e_tot[col:col + 1, :])
            slab = slice(pair * LANES, (pair + 1) * LANES)
            xp = xb[:, slab]
            zero = jnp.zeros_like(xp)
            x_bd = jnp.concatenate([jnp.where(low_half, xp, zero),
                                    jnp.where(low_half, zero, xp)], axis=0)
            st = st_s[:, slab]
            sb = st.astype(BF16)
            s_bd = jnp.concatenate([jnp.where(low_half, sb, zero),
                                    jnp.where(low_half, zero, sb)], axis=0)
            outs.append(_dot(jnp.concatenate(ms + ces, axis=1),
                             jnp.concatenate([x_bd, s_bd], axis=0)))
            grow = _dot(jnp.concatenate(bws, axis=1), x_bd)
            st_s[:, slab] = st * jnp.where(low_half, ets[0], ets[1]) + grow
        return rows, xs, jnp.concatenate(outs, axis=1)

    def finalize(rows, xs, y_scan):
        y = y_scan + dskip_ref[...] * xs
        y = y * z_ref[rows, :]
        ms = jnp.mean(y * y, axis=-1, keepdims=True)
        o_ref[rows, :] = (y * lax.rsqrt(ms + EPS) * nw_ref[...]).astype(BF16)

    stf_s[...] = jnp.zeros_like(stf_s)
    stb_s[...] = jnp.zeros_like(stb_s)

    def step(lo, hi, i, phase):
        rows_f, xs_f, yf = scan_chunk(lo + i, 0)
        if phase == "middle":
            _, _, yb = scan_chunk(lo + i, 1)
            finalize(rows_f, xs_f, yf + yb)
            return
        rows_b, xs_b, yb = scan_chunk(hi - 1 - i, 1)
        if phase == "park":
            y_s[rows_f, :] = yf
            y_s[rows_b, :] = yb
        else:
            finalize(rows_f, xs_f, yf + y_s[rows_f, :])
            finalize(rows_b, xs_b, y_s[rows_b, :] + yb)

    def segment(lo, hi):
        n = hi - lo
        half = n // 2
        if n <= 2:
            for i in range(n):
                step(lo, hi, i, "park" if i < half else ("middle" if 2 * i + 1 == n else "finish"))
            return

        def park(i, carry):
            step(lo, hi, i, "park")
            return carry

        def finish(i, carry):
            step(lo, hi, i, "finish")
            return carry

        lax.fori_loop(0, half, park, 0, unroll=4)
        if n % 2:
            step(lo, hi, half, "middle")
        lax.fori_loop(n - half, n, finish, 0, unroll=2)

    segment(0, n_ctx_chunks)
    segment(n_ctx_chunks, n_chunks)


def _ssd_call(xc, bc, cc, dt, z, sp, ctx_len):
    bsz, seq, _ = xc.shape
    q, gw, ng = SSM_CHUNK, SSM_GROUP_WIDTH, SSM_GROUPS
    n_chunks = seq // q
    grp = lambda r, w: pl.BlockSpec((None, r, w), lambda b, g: (g, 0, 0))
    col = lambda w: pl.BlockSpec((None, seq, w), lambda b, g: (b, 0, g))
    kern = functools.partial(_ssd_kernel, n_chunks=n_chunks, n_ctx_chunks=ctx_len // q)
    return pl.pallas_call(
        kern,
        grid=(bsz, ng),
        in_specs=[
            col(gw), col(SSM_STATE), col(SSM_STATE),
            pl.BlockSpec((None, seq, DT_WIDTH), lambda b, g: (b, 0, 0)),
            col(gw),
            grp(1, DT_WIDTH), grp(1, DT_WIDTH), grp(1, gw), grp(1, gw),
        ],
        out_specs=col(gw),
        out_shape=jax.ShapeDtypeStruct((bsz, seq, SSM_D_INNER), BF16),
        scratch_shapes=[
            pltpu.VMEM((n_chunks * SSM_STATE, q), F32),
            pltpu.VMEM((n_chunks * q, q), F32),
            pltpu.VMEM((seq, DT_WIDTH), F32),
            pltpu.VMEM((n_chunks * DT_WIDTH, q), F32),
            pltpu.VMEM((n_chunks * DT_WIDTH, q), F32),
            pltpu.VMEM((seq, gw), F32),
            pltpu.VMEM((SSM_STATE, gw), F32),
            pltpu.VMEM((SSM_STATE, gw), F32),
        ],
        compiler_params=_params("parallel", "arbitrary"),
        name="ssd",
    )(xc, bc, cc, dt, z, sp["dtb"], sp["alog"], sp["dskip"], sp["nw"])


def _aligned(start, multiple):
    return start if isinstance(start, int) else pl.multiple_of(start, multiple)


def _interleave(*stages):
    live = list(stages)
    while live:
        for st in list(live):
            if next(st, StopIteration) is StopIteration:
                live.remove(st)


def _gqa_kernel(q_ref, k_ref, vt_ref, o_ref, s_a, s_b, *, ctx_len, tq, skip_ctx):
    seq = k_ref.shape[0]
    kc = tq
    slots = (s_a, s_b)
    o_shift = ctx_len if skip_ctx else 0

    def scores(q_start, r, nk, out):
        q = q_ref[pl.ds(q_start, tq), r * GQA_HEAD_DIM:(r + 1) * GQA_HEAD_DIM]
        m = None
        for j in range(nk // kc):
            s = _dot_nt(k_ref[j * kc:(j + 1) * kc, :], q)
            slots[r % 2][j * kc:(j + 1) * kc, :] = s
            mj = jnp.max(s, axis=0, keepdims=True)
            m = mj if m is None else jnp.maximum(m, mj)
            yield
        out.append(m)

    def values(o_start, r, nk, m):
        l = None
        acc = None
        for j in range(nk // kc):
            p = jnp.exp2(slots[r % 2][j * kc:(j + 1) * kc, :] - m)
            lj = jnp.sum(p, axis=0, keepdims=True)
            l = lj if l is None else l + lj
            t = _dot(vt_ref[:, j * kc:(j + 1) * kc], p.astype(BF16))
            acc = t if acc is None else acc + t
            yield
        o_ref[pl.ds(o_start, tq), r * GQA_HEAD_DIM:(r + 1) * GQA_HEAD_DIM] = (
            (acc / l).T.astype(BF16))

    def tile(q_start, nk, m0, next_start):
        m = m0
        for r in range(GQA_GROUP):
            out = []
            stages = [values(_aligned(q_start - o_shift, tq), r, nk, m)]
            if r + 1 < GQA_GROUP:
                stages.append(scores(q_start, r + 1, nk, out))
            elif next_start is not None:
                stages.append(scores(next_start, 0, nk, out))
            _interleave(*stages)
            m = out[0] if out else None
        return m

    def first_scores(q_start, nk):
        out = []
        _interleave(scores(q_start, 0, nk, out))
        return out[0]

    if not skip_ctx:
        n_ctx = ctx_len // tq
        m = first_scores(0, ctx_len)
        for i in range(n_ctx):
            m = tile(i * tq, ctx_len, m, (i + 1) * tq if i + 1 < n_ctx else None)

    n_lat = (seq - ctx_len) // tq
    m = first_scores(ctx_len, seq)

    def body(i, m):
        start = pl.multiple_of(ctx_len + i * tq, tq)
        return tile(start, seq, m, pl.multiple_of(start + tq, tq))

    m = lax.fori_loop(0, n_lat - 1, body, m)
    tile(seq - tq, seq, m, None)


def _gqa_call(q, k, vt, ctx_len, tq, skip_ctx):
    bsz, seq, _ = q.shape
    gwid = GQA_GROUP * GQA_HEAD_DIM
    out_rows = seq - ctx_len if skip_ctx else seq
    kern = functools.partial(_gqa_kernel, ctx_len=ctx_len, tq=tq, skip_ctx=skip_ctx)
    return pl.pallas_call(
        kern,
        grid=(bsz, GQA_KV_HEADS),
        in_specs=[pl.BlockSpec((None, seq, gwid), lambda b, g: (b, 0, g)),
                  pl.BlockSpec((None, seq, GQA_HEAD_DIM), lambda b, g: (b, 0, g)),
                  pl.BlockSpec((None, GQA_HEAD_DIM, seq), lambda b, g: (b, g, 0))],
        out_specs=pl.BlockSpec((None, out_rows, gwid), lambda b, g: (b, 0, g)),
        out_shape=jax.ShapeDtypeStruct((bsz, out_rows, GQA_WIDTH), BF16),
        scratch_shapes=[pltpu.VMEM((seq, tq), F32), pltpu.VMEM((seq, tq), F32)],
        compiler_params=_params("parallel", "parallel"),
        name="gqa",
    )(q, k, vt)


def _diff_kernel(q_ref, k_ref, v_ref, lam_ref, nw_ref, o_ref, s_a, s_b, m_a, m_b, *, ctx_len, tq,
                 skip_ctx, lam_init):
    lv = lam_ref[...]
    lam = (jnp.exp(jnp.sum(lv[0:1] * lv[1:2], axis=-1, keepdims=True))
           - jnp.exp(jnp.sum(lv[2:3] * lv[3:4], axis=-1, keepdims=True)) + lam_init)
    seq = k_ref.shape[0]
    slots = (s_a, s_b)
    maxes = (m_a, m_b)
    o_shift = ctx_len if skip_ctx else 0

    def scores(q_start, nk, slot):
        q = q_ref[pl.ds(q_start, tq), :]
        lane = lax.broadcasted_iota(jnp.int32, q.shape, 1)
        zero = jnp.zeros_like(q)
        q2 = jnp.concatenate([jnp.where(lane < DIFF_HEAD_DIM, q, zero),
                              jnp.where(lane >= DIFF_HEAD_DIM, q, zero)], axis=0)
        s = _dot_nt(q2, k_ref[0:nk, :])
        slots[slot][:, 0:nk] = s
        maxes[slot][...] = jnp.max(s, axis=-1, keepdims=True)

    def finish(nk, slot, o_start):
        s = slots[slot][:, 0:nk]
        p = jnp.exp2(s - maxes[slot][...])
        l = jnp.sum(p, axis=-1, keepdims=True)
        l1, l2 = l[0:tq], l[tq:2 * tq]
        w = p[0:tq] - p[tq:2 * tq] * (lam * l1 / l2)
        o = _dot(w.astype(BF16), v_ref[0:nk, :]) * (1.0 / l1)
        ms = jnp.mean(o * o, axis=-1, keepdims=True)
        o_ref[pl.ds(o_start, tq), :] = (
            o * lax.rsqrt(ms + EPS) * nw_ref[...] * (1.0 - lam_init)).astype(BF16)

    if not skip_ctx:
        for i in range(ctx_len // tq):
            scores(i * tq, ctx_len, 0)
            finish(ctx_len, 0, i * tq)

    def pair(t0, last):
        scores(_aligned(t0 + tq, tq), seq, 1)
        finish(seq, 0, _aligned(t0 - o_shift, tq))
        if not last:
            scores(_aligned(t0 + 2 * tq, tq), seq, 0)
        finish(seq, 1, _aligned(t0 + tq - o_shift, tq))

    n_pairs = (seq - ctx_len) // (2 * tq)
    scores(ctx_len, seq, 0)
    for j in range(n_pairs):
        pair(ctx_len + 2 * j * tq, j == n_pairs - 1)


def _diff_call(q, k, v, lam_p, nw, lam_init, ctx_len, tq, skip_ctx):
    bsz, seq, _ = q.shape
    out_rows = seq - ctx_len if skip_ctx else seq
    kern = functools.partial(_diff_kernel, ctx_len=ctx_len, tq=tq, skip_ctx=skip_ctx,
                             lam_init=lam_init)
    head = pl.BlockSpec((None, seq, LANES), lambda b, h: (b, 0, h))
    return pl.pallas_call(
        kern,
        grid=(bsz, DIFF_HEADS),
        in_specs=[head, head, head, _const_spec(lam_p.shape), _const_spec(nw.shape)],
        out_specs=pl.BlockSpec((None, out_rows, LANES), lambda b, h: (b, 0, h)),
        out_shape=jax.ShapeDtypeStruct((bsz, out_rows, DIFF_WIDTH), BF16),
        scratch_shapes=[pltpu.VMEM((2 * tq, seq), F32), pltpu.VMEM((2 * tq, seq), F32),
                        pltpu.VMEM((2 * tq, 1), F32), pltpu.VMEM((2 * tq, 1), F32)],
        compiler_params=_params("parallel", "parallel"),
        name="diff_attn",
    )(q, k, v, lam_p, nw)


def _merge_kernel(x_ref, mod_ref, ys_ref, yg_ref, yd_ref, wg_ref, bg_ref, ws_ref, wq_ref, wd_ref,
                  wo_ref, lg_ref, lb_ref, o_ref, *, n_ctx_rows):
    d = D_MODEL
    x = x_ref[...]
    h = _modulated(x_ref, mod_ref, 0, n_ctx_rows)
    m = None
    for i, (y_ref, w_ref) in enumerate(((ys_ref, ws_ref), (yg_ref, wq_ref), (yd_ref, wd_ref))):
        gate = _sigmoid(_dot(h, wg_ref[:, i * d:(i + 1) * d]) + bg_ref[:, i * d:(i + 1) * d])
        term = gate * _dot(y_ref[...], w_ref[...])
        m = term if m is None else m + term
    y = _dot(m.astype(BF16), wo_ref[...])
    r = DEEPNORM_ALPHA * x + _mod_row(mod_ref, 2, n_ctx_rows, x.shape[0]) * y
    o_ref[...] = _layer_norm(r, lg_ref[...], lb_ref[...])


def _merge_call(xs, mods, ys, yg, yd, lw, n_ctx_rows, tm, row_off):
    bsz, seq, d = xs.shape
    full = lambda w: pl.BlockSpec((None, tm, w), lambda b, t: (b, t + row_off, 0))
    tok = lambda w: pl.BlockSpec((None, tm, w), lambda b, t: (b, t, 0))
    consts = [lw["w_gate"], lw["b_gate"], lw["w_ssm_out"], lw["w_gqa_out"], lw["w_diff_out"],
              lw["w_o"], lw["ln1_g"], lw["ln1_b"]]
    return pl.pallas_call(
        functools.partial(_merge_kernel, n_ctx_rows=n_ctx_rows),
        grid=(bsz, seq // tm - row_off),
        in_specs=[full(d), _mod_spec(d),
                  full(SSM_D_INNER), tok(GQA_WIDTH), tok(DIFF_WIDTH)]
                 + [_const_spec(a.shape) for a in consts],
        out_specs=tok(d),
        out_shape=jax.ShapeDtypeStruct((bsz, seq - row_off * tm, d), F32),
        compiler_params=_params("parallel", "parallel"),
        name="merge",
    )(xs, mods, ys, yg, yd, *consts)


def _ffn_kernel(x_ref, mod_ref, wi_ref, wo_ref, lg_ref, lb_ref, o_ref, *, n_ctx_rows):
    x = x_ref[...]
    h = _modulated(x_ref, mod_ref, 3, n_ctx_rows)
    acc = None
    for c0 in range(0, FFN_HIDDEN, FFN_CHUNK):
        a = _dot(h, wi_ref[:, c0:c0 + FFN_CHUNK])
        b = _dot(h, wi_ref[:, FFN_HIDDEN + c0:FFN_HIDDEN + c0 + FFN_CHUNK])
        u = (_silu(a) * b).astype(BF16)
        part = _dot(u, wo_ref[c0:c0 + FFN_CHUNK, :])
        acc = part if acc is None else acc + part
    r = DEEPNORM_ALPHA * x + _mod_row(mod_ref, 5, n_ctx_rows, x.shape[0]) * acc
    o_ref[...] = _layer_norm(r, lg_ref[...], lb_ref[...])


def _ffn_call(xs, mods, lw, n_ctx_rows, tm):
    bsz, seq, d = xs.shape
    tok = pl.BlockSpec((None, tm, d), lambda b, t: (b, t, 0))
    consts = [lw["ffn_w_in"], lw["ffn_w_out"], lw["ln2_g"], lw["ln2_b"]]
    return pl.pallas_call(
        functools.partial(_ffn_kernel, n_ctx_rows=n_ctx_rows),
        grid=(bsz, seq // tm),
        in_specs=[tok, _mod_spec(d)]
                 + [_const_spec(a.shape) for a in consts],
        out_specs=tok,
        out_shape=jax.ShapeDtypeStruct((bsz, seq, d), F32),
        compiler_params=_params("parallel", "parallel"),
        name="ffn",
    )(xs, mods, *consts)


def _rope_tables(ctx_len, lat_len, head_dim, half_sign_period):
    t = jnp.arange(lat_len, dtype=jnp.int32)
    d_axis = head_dim // 2
    inv_freq = ROPE_THETA ** (-jnp.arange(0, d_axis, 2, dtype=F32) / d_axis)
    ang_r = (t // GRID_W).astype(F32)[:, None] * inv_freq
    ang_c = (t % GRID_W).astype(F32)[:, None] * inv_freq
    ang = jnp.concatenate([ang_r, ang_r, ang_c, ang_c], axis=-1)
    cos, sin = jnp.cos(ang), jnp.sin(ang)
    reps = LANES // head_dim
    cos, sin = jnp.tile(cos, (1, reps)), jnp.tile(sin, (1, reps))
    lane = jnp.arange(LANES)
    sign = jnp.where((lane % half_sign_period) < half_sign_period // 2, -1.0, 1.0).astype(F32)
    cos = jnp.concatenate([jnp.ones((ctx_len, LANES), F32), cos], axis=0)
    sin = jnp.concatenate([jnp.zeros((ctx_len, LANES), F32), sin * sign], axis=0)
    return cos, sin


def _group_rows(v, width):
    return v.reshape(SSM_GROUPS, 1, width)


def _ssm_params(dt_bias, a_log, d_skip, norm_w):
    gh, ng = SSM_GROUP_HEADS, SSM_GROUPS

    def per_dir(v):
        r = jnp.transpose(v.reshape(2, ng, gh), (1, 0, 2)).reshape(ng, 1, 2 * gh)
        return jnp.pad(r, ((0, 0), (0, 0), (0, DT_WIDTH - 2 * gh)))

    return {
        "dtb": per_dir(dt_bias), "alog": per_dir(a_log),
        "dskip": _group_rows(jnp.repeat(d_skip, SSM_HEAD_DIM), SSM_GROUP_WIDTH),
        "nw": _group_rows(norm_w, SSM_GROUP_WIDTH),
    }


def kernel(x, c, ctx, c_ctx, ada_w, ada_b, w_in, b_gate, ssm_conv_w, ssm_conv_b, ssm_dt_bias,
           ssm_a_log, ssm_d, ssm_norm_w, w_ssm_out, gqa_q_norm, gqa_k_norm, w_gqa_out, diff_lambda,
           diff_norm_w, w_diff_out, w_o, ln1_g, ln1_b, ffn_w_in, ffn_w_out, ln2_g, ln2_b):
    bsz, lat_len, d = x.shape
    ctx_len = ctx.shape[1]
    depth = w_in.shape[0]
    assert d == D_MODEL and depth == DEPTH and w_in.shape[2] == IN_WIDTH
    assert lat_len % GRID_W == 0 and lat_len % SSM_CHUNK == 0 and ctx_len % SSM_CHUNK == 0
    tq = math.gcd(256, ctx_len)
    seq = ctx_len + lat_len
    tm = _row_tile(seq, tq)
    tm_lat = _row_tile(lat_len, tq)
    tm_ssm = _row_tile(seq, SSM_CHUNK, SSM_ROW_TILE)
    assert (lat_len // tq) % 2 == 0

    rows = -(-(bsz + 1) // SUBLANES) * SUBLANES
    cvec = jnp.concatenate([c, c_ctx[None], jnp.zeros((rows - bsz - 1, d), F32)], axis=0)
    ada = _ada_call(cvec, ada_w, ada_b)

    rope_g = _rope_tables(ctx_len, lat_len, GQA_HEAD_DIM, GQA_HEAD_DIM // 2)
    rope_d = _rope_tables(ctx_len, lat_len, DIFF_HEAD_DIM, DIFF_HEAD_DIM // 2)
    tabs = (*rope_g, *rope_d)

    xs = jnp.concatenate([ctx, x], axis=1)
    pad = jnp.zeros((d, DT_WIDTH - 2 * SSM_HEADS), BF16)
    for i in range(depth):
        need_ctx = i < depth - 1
        lam_init = 0.8 - 0.6 * math.exp(-0.3 * i)
        mod_l = ada[i, :bsz].reshape(bsz, 1, 6, d)
        mod_c = jnp.broadcast_to(ada[i, bsz].reshape(1, 1, 6, d), (bsz, 1, 6, d))
        mods = jnp.concatenate([mod_c, mod_l], axis=1)
        w = w_in[i].astype(BF16)
        w_ssm = jnp.concatenate([w[:, :OFF_GQ], pad], axis=1)
        w_attn = w[:, OFF_GQ:OFF_GATE]
        lw = {
            "w_gate": w[:, OFF_GATE:], "b_gate": b_gate[i][None],
            "w_ssm_out": w_ssm_out[i].astype(BF16), "w_gqa_out": w_gqa_out[i].astype(BF16),
            "w_diff_out": w_diff_out[i].astype(BF16), "w_o": w_o[i].astype(BF16),
            "ln1_g": ln1_g[i][None], "ln1_b": ln1_b[i][None],
            "ffn_w_in": ffn_w_in[i].astype(BF16), "ffn_w_out": ffn_w_out[i].astype(BF16),
            "ln2_g": ln2_g[i][None], "ln2_b": ln2_b[i][None],
        }
        sp = _ssm_params(ssm_dt_bias[i], ssm_a_log[i], ssm_d[i], ssm_norm_w[i])
        conv_wb = jnp.concatenate([ssm_conv_w[i], ssm_conv_b[i][None],
                                   jnp.zeros((SUBLANES - SSM_CONV - 1, SSM_CONV_DIM), F32)], axis=0)

        z, xc, bc, cc, dt = _inproj_ssm_call(xs, mods, w_ssm, conv_wb, ctx_len, tm_ssm)
        gq, gk, gvt, dq, dk, dv = _inproj_attn_call(
            xs, mods, w_attn, gqa_q_norm[i][None], gqa_k_norm[i][None], tabs, ctx_len, tm)
        y_ssm = _ssd_call(xc, bc, cc, dt, z, sp, ctx_len)
        y_gqa = _gqa_call(gq, gk, gvt, ctx_len, tq, not need_ctx)
        y_diff = _diff_call(dq, dk, dv, diff_lambda[i], diff_norm_w[i][None], lam_init, ctx_len, tq,
                            not need_ctx)
        if need_ctx:
            x1 = _merge_call(xs, mods, y_ssm, y_gqa, y_diff, lw, ctx_len, tm, 0)
            xs = _ffn_call(x1, mods, lw, ctx_len, tm)
        else:
            x1 = _merge_call(xs, mods, y_ssm, y_gqa, y_diff, lw, 0, tq, ctx_len // tq)
            xs = _ffn_call(x1, mods, lw, 0, tm_lat)
    return xs
```

```python
import functools
import math

import jax
import jax.numpy as jnp
from jax import lax
from jax.experimental import pallas as pl
from jax.experimental.pallas import tpu as pltpu

F32 = jnp.float32
BF16 = jnp.bfloat16

D_MODEL = 1024
DEPTH = 4
GRID_W = 64
ROPE_THETA = 10000.0
EPS = 1e-6

SSM_D_INNER = 2048
SSM_HEAD_DIM = 64
SSM_HEADS = 32
SSM_GROUPS = 4
SSM_STATE = 128
SSM_CONV = 5
SSM_CHUNK = 128
SSM_BC = SSM_GROUPS * SSM_STATE
SSM_CONV_DIM = SSM_D_INNER + 2 * SSM_BC
SSM_GROUP_WIDTH = SSM_D_INNER // SSM_GROUPS
SSM_GROUP_HEADS = SSM_HEADS // SSM_GROUPS

GQA_HEAD_DIM = 128
GQA_HEADS = 8
GQA_KV_HEADS = 2
GQA_GROUP = GQA_HEADS // GQA_KV_HEADS
GQA_WIDTH = GQA_HEADS * GQA_HEAD_DIM
GQA_KV_WIDTH = GQA_KV_HEADS * GQA_HEAD_DIM

DIFF_HEAD_DIM = 64
DIFF_HEADS = 8
DIFF_WIDTH = DIFF_HEADS * 2 * DIFF_HEAD_DIM

N_BRANCHES = 3
FFN_HIDDEN = 2816
FFN_CHUNK = 256

DEEPNORM_ALPHA = (2 * DEPTH) ** 0.25

LANES = 128
SUBLANES = 8
DT_WIDTH = LANES
SSM_CONV_TILES = SSM_CONV_DIM // LANES
LOG2E = 1.4426950408889634
VMEM_LIMIT = 56 * 1024 * 1024
MAX_ROW_TILE = 768
CONV_HALO = 16
SSM_ROW_TILE = 384

OFF_Z = 0
OFF_XBC = OFF_Z + SSM_D_INNER
OFF_DT = OFF_XBC + SSM_CONV_DIM
OFF_GQ = OFF_DT + 2 * SSM_HEADS
OFF_GK = OFF_GQ + GQA_WIDTH
OFF_GV = OFF_GK + GQA_KV_WIDTH
OFF_DQ = OFF_GV + GQA_KV_WIDTH
OFF_DK = OFF_DQ + DIFF_WIDTH
OFF_DV = OFF_DK + DIFF_WIDTH
OFF_GATE = OFF_DV + DIFF_WIDTH
IN_WIDTH = OFF_GATE + N_BRANCHES * D_MODEL


def _dot(a, b):
    return jnp.dot(a, b, preferred_element_type=F32)


def _dot_nt(a, b):
    return lax.dot_general(a, b, (((1,), (1,)), ((), ())), preferred_element_type=F32)


def _split3(a):
    hi = a.astype(BF16)
    r = a - hi.astype(F32)
    mid = r.astype(BF16)
    lo = (r - mid.astype(F32)).astype(BF16)
    return hi, mid, lo


def _dot_exact_rhs(a, sel):
    hi, mid, lo = _split3(a)
    return _dot(hi, sel) + _dot(mid, sel) + _dot(lo, sel)


def _sigmoid(x):
    return 1.0 / (1.0 + jnp.exp(-x))


def _silu(x):
    return x * _sigmoid(x)


def _layer_norm(r, g, b):
    mu = jnp.mean(r, axis=-1, keepdims=True)
    d = r - mu
    var = jnp.mean(d * d, axis=-1, keepdims=True)
    return d * lax.rsqrt(var + EPS) * g + b


def _params(*sem):
    return pltpu.CompilerParams(dimension_semantics=sem, vmem_limit_bytes=VMEM_LIMIT)


def _const_spec(shape):
    nd = len(shape)
    return pl.BlockSpec(shape, lambda *_: (0,) * nd, pipeline_mode=pl.Buffered(1))


def _ada_kernel(c_ref, w_ref, b_ref, o_ref):
    sc = _silu(c_ref[...]).astype(BF16)
    o_ref[...] = _dot(sc, w_ref[...].astype(BF16)) + b_ref[...]


def _ada_call(cvec, ada_w, ada_b):
    rows = cvec.shape[0]
    depth, d, n = ada_w.shape
    tn = n // 4
    return pl.pallas_call(
        _ada_kernel,
        grid=(depth, n // tn),
        in_specs=[
            pl.BlockSpec((rows, d), lambda i, j: (0, 0)),
            pl.BlockSpec((None, d, tn), lambda i, j: (i, 0, j)),
            pl.BlockSpec((None, 1, tn), lambda i, j: (i, 0, j)),
        ],
        out_specs=pl.BlockSpec((None, rows, tn), lambda i, j: (i, 0, j)),
        out_shape=jax.ShapeDtypeStruct((depth, rows, n), F32),
        compiler_params=_params("arbitrary", "arbitrary"),
        name="ada",
    )(cvec, ada_w, ada_b.reshape(depth, 1, n))


def _mod_row(mod_ref, row, n_ctx_rows, n, first=None):
    lat = mod_ref[1, row:row + 1, :]
    if n_ctx_rows == 0:
        return lat
    if first is None:
        first = pl.program_id(1) * n
    is_ctx = first + lax.broadcasted_iota(jnp.int32, (n, 1), 0) < n_ctx_rows
    return jnp.where(is_ctx, mod_ref[0, row:row + 1, :], lat)


def _modulated(x_ref, mod_ref, shift_row, n_ctx_rows, first=None):
    x = x_ref[...]
    shift = _mod_row(mod_ref, shift_row, n_ctx_rows, x.shape[0], first)
    scale = _mod_row(mod_ref, shift_row + 1, n_ctx_rows, x.shape[0], first)
    return (x * (1.0 + scale) + shift).astype(BF16)


def _inproj_ssm_kernel(x_ref, xp_ref, xn_ref, mod_ref, w_ref, cw_ref, z_ref, xc_ref, bc_ref, cc_ref,
                       dt_ref, u_scr, *, n_ctx_rows, seq):
    tm = x_ref.shape[0]
    halo = CONV_HALO
    first = pl.program_id(1) * tm
    h = _modulated(x_ref, mod_ref, 0, n_ctx_rows)
    h_all = jnp.concatenate([_modulated(xp_ref, mod_ref, 0, n_ctx_rows, first - halo), h,
                             _modulated(xn_ref, mod_ref, 0, n_ctx_rows, first + tm)], axis=0)
    step = 512

    prev_ok = jnp.logical_and(first != 0, first != n_ctx_rows)
    next_ok = jnp.logical_and(first + tm != seq, first + tm != n_ctx_rows)
    split = n_ctx_rows % tm
    center = SSM_CONV // 2

    def conv(j, lo, n, valid=None):
        cw = cw_ref[:, j * LANES:(j + 1) * LANES]
        acc = None
        for k in range(SSM_CONV):
            win = u_scr[j, halo + lo + k - center:halo + lo + k - center + n, :]
            if valid is not None and k != center:
                win = jnp.where(valid[k], win, 0.0)
            term = win * cw[k:k + 1, :]
            acc = term if acc is None else acc + term
        return _silu(acc + cw[SSM_CONV:SSM_CONV + 1, :])

    def dest(j):
        c = j * LANES
        if c < SSM_D_INNER:
            return xc_ref, slice(c, c + LANES)
        if c < SSM_D_INNER + SSM_BC:
            return bc_ref, slice(c - SSM_D_INNER, c - SSM_D_INNER + LANES)
        return cc_ref, slice(c - SSM_D_INNER - SSM_BC, c - SSM_D_INNER - SSM_BC + LANES)

    per = step // LANES
    for ci in range(SSM_CONV_DIM // step):
        c0 = OFF_XBC + ci * step
        u = _dot(h_all, w_ref[:, c0:c0 + step])
        for j in range(per):
            cols = slice(j * LANES, (j + 1) * LANES)
            t = ci * per + j
            u_scr[t, 0:halo, :] = jnp.where(prev_ok, u[0:halo, cols], 0.0)
            u_scr[t, halo:halo + tm, :] = u[halo:halo + tm, cols]
            u_scr[t, halo + tm:halo + tm + halo, :] = jnp.where(next_ok, u[halo + tm:, cols], 0.0)
        for j in range(per):
            ref, cols = dest(ci * per + j)
            ref[:, cols] = conv(ci * per + j, 0, tm)
    for c0 in range(0, SSM_D_INNER, step):
        z_ref[:, c0:c0 + step] = _silu(_dot(h, w_ref[:, OFF_Z + c0:OFF_Z + c0 + step]))
    dt_ref[...] = _dot(h, w_ref[:, OFF_DT:OFF_DT + DT_WIDTH])
    if split:
        @pl.when(pl.program_id(1) == n_ctx_rows // tm)
        def _():
            lo, n = split - SUBLANES, 2 * SUBLANES
            r = lo + lax.broadcasted_iota(jnp.int32, (n, LANES), 0)
            valid = [(r < split) == (r + k - center < split) for k in range(SSM_CONV)]
            for t in range(SSM_CONV_TILES):
                ref, cols = dest(t)
                ref[lo:lo + n, cols] = conv(t, lo, n, valid)


def _mod_spec(d):
    return pl.BlockSpec((None, 2, 6, d), lambda b, t: (b, 0, 0, 0))


def _row_tile(rows, unit, limit=None):
    best = unit
    for k in range(1, (limit or MAX_ROW_TILE) // unit + 1):
        if rows % (k * unit) == 0:
            best = k * unit
    return best


def _inproj_ssm_call(xs, mods, w_ssm, conv_wb, ctx_len, tm):
    bsz, seq, d = xs.shape
    halo = CONV_HALO
    per = tm // halo
    last = seq // halo - 1
    tok = lambda w: pl.BlockSpec((None, tm, w), lambda b, t: (b, t, 0))
    out = lambda w: jax.ShapeDtypeStruct((bsz, seq, w), F32)
    return pl.pallas_call(
        functools.partial(_inproj_ssm_kernel, n_ctx_rows=ctx_len, seq=seq),
        grid=(bsz, seq // tm),
        in_specs=[tok(d),
                  pl.BlockSpec((None, halo, d), lambda b, t: (b, jnp.maximum(t * per - 1, 0), 0)),
                  pl.BlockSpec((None, halo, d), lambda b, t: (b, jnp.minimum((t + 1) * per, last), 0)),
                  _mod_spec(d), _const_spec(w_ssm.shape), _const_spec(conv_wb.shape)],
        out_specs=[tok(SSM_D_INNER), tok(SSM_D_INNER), tok(SSM_BC), tok(SSM_BC), tok(DT_WIDTH)],
        out_shape=[out(SSM_D_INNER), out(SSM_D_INNER), out(SSM_BC), out(SSM_BC), out(DT_WIDTH)],
        scratch_shapes=[pltpu.VMEM((SSM_CONV_TILES, tm + 2 * halo, LANES), F32)],
        compiler_params=_params("parallel", "parallel"),
        name="inproj_ssm",
    )(xs, xs, xs, mods, w_ssm, conv_wb)


def _rope(u, cos, sin_signed, half):
    lane = lax.broadcasted_iota(jnp.int32, u.shape, 1)
    fwd = pltpu.roll(u, LANES - half, axis=1)
    bwd = pltpu.roll(u, half, axis=1)
    partner = jnp.where((lane % (2 * half)) < half, fwd, bwd)
    return u * cos + partner * sin_signed


def _rms_heads(u, g):
    ms = jnp.mean(u * u, axis=-1, keepdims=True)
    return u * lax.rsqrt(ms + EPS) * g


def _inproj_attn_kernel(x_ref, mod_ref, w_ref, qn_ref, kn_ref, cg_ref, sg_ref, cd_ref, sd_ref,
                        gq_ref, gk_ref, gv_ref, dq_ref, dk_ref, dv_ref, *, n_ctx_rows):
    h = _modulated(x_ref, mod_ref, 0, n_ctx_rows)
    base = OFF_GQ
    cg, sg = cg_ref[...], sg_ref[...]
    cd, sd = cd_ref[...], sd_ref[...]
    hd = GQA_HEAD_DIM

    def heads(off, width):
        step = min(512, width)
        for c0 in range(0, width, step):
            u = _dot(h, w_ref[:, off - base + c0:off - base + c0 + step])
            for j in range(step // LANES):
                yield (c0 // LANES + j) * LANES, u[:, j * LANES:(j + 1) * LANES]

    gqa_scale = GQA_HEAD_DIM ** -0.5 * LOG2E
    for c, u in heads(OFF_GQ, GQA_WIDTH):
        u = _rope(_rms_heads(u, qn_ref[...]), cg, sg, GQA_HEAD_DIM // 4)
        gq_ref[:, c:c + hd] = (u * gqa_scale).astype(BF16)
    for c, u in heads(OFF_GK, GQA_KV_WIDTH):
        u = _rope(_rms_heads(u, kn_ref[...]), cg, sg, GQA_HEAD_DIM // 4)
        gk_ref[:, c:c + hd] = u.astype(BF16)
    for c, u in heads(OFF_GV, GQA_KV_WIDTH):
        gv_ref[c:c + hd, :] = u.T.astype(BF16)
    scale = DIFF_HEAD_DIM ** -0.5 * LOG2E
    for c, u in heads(OFF_DQ, DIFF_WIDTH):
        dq_ref[:, c:c + LANES] = (_rope(u, cd, sd, DIFF_HEAD_DIM // 4) * scale).astype(BF16)
    for c, u in heads(OFF_DK, DIFF_WIDTH):
        dk_ref[:, c:c + LANES] = _rope(u, cd, sd, DIFF_HEAD_DIM // 4).astype(BF16)
    for c0 in range(0, DIFF_WIDTH, 512):
        dv_ref[:, c0:c0 + 512] = _dot(
            h, w_ref[:, OFF_DV - base + c0:OFF_DV - base + c0 + 512]).astype(BF16)


def _inproj_attn_call(xs, mods, w_attn, qn, kn, tabs, ctx_len, tm):
    bsz, seq, d = xs.shape
    tok = lambda w: pl.BlockSpec((None, tm, w), lambda b, t: (b, t, 0))
    tab = pl.BlockSpec((tm, LANES), lambda b, t: (t, 0))
    tr = lambda w: pl.BlockSpec((None, w, tm), lambda b, t: (b, 0, t))
    row = lambda w: jax.ShapeDtypeStruct((bsz, seq, w), BF16)
    col = lambda w: jax.ShapeDtypeStruct((bsz, w, seq), BF16)
    return pl.pallas_call(
        functools.partial(_inproj_attn_kernel, n_ctx_rows=ctx_len),
        grid=(bsz, seq // tm),
        in_specs=[tok(d), _mod_spec(d), _const_spec(w_attn.shape),
                  _const_spec(qn.shape), _const_spec(kn.shape), tab, tab, tab, tab],
        out_specs=[tok(GQA_WIDTH), tok(GQA_KV_WIDTH), tr(GQA_KV_WIDTH),
                   tok(DIFF_WIDTH), tok(DIFF_WIDTH), tok(DIFF_WIDTH)],
        out_shape=[row(GQA_WIDTH), row(GQA_KV_WIDTH), col(GQA_KV_WIDTH),
                   row(DIFF_WIDTH), row(DIFF_WIDTH), row(DIFF_WIDTH)],
        compiler_params=_params("parallel", "parallel"),
        name="inproj_attn",
    )(xs, mods, w_attn, qn, kn, *tabs)


def _ssd_kernel(xs_s, b_ref, c_s, dtr_ref, z_ref, dtb_ref, alog_ref, dskip_ref, nw_ref, o_ref,
                bt_s, cb_s, acs_s, acst_s, dtt_s, y_s, stf_s, stb_s, *, n_chunks, n_ctx_chunks):
    q = SSM_CHUNK
    gh = SSM_GROUP_HEADS
    hp = SSM_HEAD_DIM
    g = pl.program_id(1)

    row_i = lax.broadcasted_iota(jnp.int32, (q, q), 0)
    col_i = lax.broadcasted_iota(jnp.int32, (q, q), 1)
    tri_lo = (col_i <= row_i).astype(BF16)
    tri_up = (col_i >= row_i).astype(BF16)
    src = jnp.where(col_i < gh, g * gh + col_i, SSM_HEADS + g * gh + col_i - gh)
    pick = ((row_i == src) & (col_i < 2 * gh)).astype(BF16)
    a_neg = -jnp.exp(alog_ref[...])
    dt_bias = dtb_ref[...]

    seq = n_chunks * q
    raw = _dot_exact_rhs(dtr_ref[...], pick) + dt_bias
    dtv = jnp.maximum(raw, 0.0) + jnp.log1p(jnp.exp(-jnp.abs(raw)))
    a_cols = jnp.broadcast_to(a_neg, (q, q)).T
    for c in range(n_chunks):
        rows = slice(c * q, (c + 1) * q)
        dt_t = dtv[rows, :].T
        dtt_s[rows, :] = dt_t
        acst_s[rows, :] = dt_t * a_cols
        bc = b_ref[rows, :]
        bt_s[rows, :] = bc.T
        cb_s[rows, :] = _dot_nt(c_s[rows, :].astype(BF16), bc.astype(BF16))
    hi, mid, lo = _split3(acst_s[...])
    prefix = _dot(hi, tri_up) + _dot(mid, tri_up) + _dot(lo, tri_up)
    suffix = _dot(hi, tri_lo) + _dot(mid, tri_lo) + _dot(lo, tri_lo)
    head_col = lax.broadcasted_iota(jnp.int32, (seq, q), 0) & (q - 1)
    acst_all = jnp.where(head_col < gh, prefix, suffix)
    acst_s[...] = acst_all
    for c in range(n_chunks):
        rows = slice(c * q, (c + 1) * q)
        acs_s[rows, :] = acst_all[rows, :].T

    low_half = col_i < hp

    def scan_chunk(c, direction):
        st_s = (stf_s, stb_s)[direction]
        r0 = _aligned(c * q, q)
        rows = pl.ds(r0, q)
        xs = xs_s[rows, :]
        xb = xs.astype(BF16)
        cc = c_s[rows, :]
        acs = acs_s[rows, :]
        acst = acst_s[rows, :]
        dtt = dtt_s[rows, :]
        btf = bt_s[rows, :]
        cb = cb_s[rows, :]
        end = q - 1 if direction == 0 else 0
        tot = acst[:, end:end + 1]
        mine = (row_i >= gh * direction) & (row_i < gh * (direction + 1))
        w_out = dtt * jnp.exp(jnp.where(mine, tot - acst, 0.0))
        e_tot = jnp.exp(tot)
        keep = (row_i >= col_i) if direction == 0 else (col_i >= row_i)
        acs2 = acs * LOG2E
        src2 = (acst - jnp.log(dtt)) * LOG2E
        outs = []
        for pair in range(gh // 2):
            ms, ces, bws, ets = [], [], [], []
            for hh in range(2):
                col = gh * direction + 2 * pair + hh
                a_l = jnp.broadcast_to(acs2[:, col:col + 1], (q, q))
                decay = jnp.where(keep, jnp.exp2(a_l - src2[col:col + 1, :]), 0.0)
                ms.append((cb * decay).astype(BF16))
                ces.append((cc * jnp.exp2(a_l)).astype(BF16))
                bws.append((btf * w_out[col:col + 1, :]).astype(BF16))
                ets.append(e_tot[col:col + 1, :])
            slab = slice(pair * LANES, (pair + 1) * LANES)
            xp = xb[:, slab]
            zero = jnp.zeros_like(xp)
            x_bd = jnp.concatenate([jnp.where(low_half, xp, zero),
                                    jnp.where(low_half, zero, xp)], axis=0)
            st = st_s[:, slab]
            sb = st.astype(BF16)
            s_bd = jnp.concatenate([jnp.where(low_half, sb, zero),
                                    jnp.where(low_half, zero, sb)], axis=0)
            outs.append(_dot(jnp.concatenate(ms + ces, axis=1),
                             jnp.concatenate([x_bd, s_bd], axis=0)))
            grow = _dot(jnp.concatenate(bws, axis=1), x_bd)
            st_s[:, slab] = st * jnp.where(low_half, ets[0], ets[1]) + grow
        return rows, xs, jnp.concatenate(outs, axis=1)

    def finalize(rows, xs, y_scan):
        y = y_scan + dskip_ref[...] * xs
        y = y * z_ref[rows, :]
        ms = jnp.mean(y * y, axis=-1, keepdims=True)
        o_ref[rows, :] = (y * lax.rsqrt(ms + EPS) * nw_ref[...]).astype(BF16)

    stf_s[...] = jnp.zeros_like(stf_s)
    stb_s[...] = jnp.zeros_like(stb_s)

    def step(lo, hi, i, phase):
        rows_f, xs_f, yf = scan_chunk(lo + i, 0)
        if phase == "middle":
            _, _, yb = scan_chunk(lo + i, 1)
            finalize(rows_f, xs_f, yf + yb)
            return
        rows_b, xs_b, yb = scan_chunk(hi - 1 - i, 1)
        if phase == "park":
            y_s[rows_f, :] = yf
            y_s[rows_b, :] = yb
        else:
            finalize(rows_f, xs_f, yf + y_s[rows_f, :])
            finalize(rows_b, xs_b, y_s[rows_b, :] + yb)

    def segment(lo, hi):
        n = hi - lo
        half = n // 2
        if n <= 2:
            for i in range(n):
                step(lo, hi, i, "park" if i < half else ("middle" if 2 * i + 1 == n else "finish"))
            return

        def park(i, carry):
            step(lo, hi, i, "park")
            return carry

        def finish(i, carry):
            step(lo, hi, i, "finish")
            return carry

        lax.fori_loop(0, half, park, 0, unroll=4)
        if n % 2:
            step(lo, hi, half, "middle")
        lax.fori_loop(n - half, n, finish, 0, unroll=2)

    segment(0, n_ctx_chunks)
    segment(n_ctx_chunks, n_chunks)


def _ssd_call(xc, bc, cc, dt, z, sp, ctx_len):
    bsz, seq, _ = xc.shape
    q, gw, ng = SSM_CHUNK, SSM_GROUP_WIDTH, SSM_GROUPS
    n_chunks = seq // q
    grp = lambda r, w: pl.BlockSpec((None, r, w), lambda b, g: (g, 0, 0))
    col = lambda w: pl.BlockSpec((None, seq, w), lambda b, g: (b, 0, g))
    kern = functools.partial(_ssd_kernel, n_chunks=n_chunks, n_ctx_chunks=ctx_len // q)
    return pl.pallas_call(
        kern,
        grid=(bsz, ng),
        in_specs=[
            col(gw), col(SSM_STATE), col(SSM_STATE),
            pl.BlockSpec((None, seq, DT_WIDTH), lambda b, g: (b, 0, 0)),
            col(gw),
            grp(1, DT_WIDTH), grp(1, DT_WIDTH), grp(1, gw), grp(1, gw),
        ],
        out_specs=col(gw),
        out_shape=jax.ShapeDtypeStruct((bsz, seq, SSM_D_INNER), BF16),
        scratch_shapes=[
            pltpu.VMEM((n_chunks * SSM_STATE, q), F32),
            pltpu.VMEM((n_chunks * q, q), F32),
            pltpu.VMEM((seq, DT_WIDTH), F32),
            pltpu.VMEM((n_chunks * DT_WIDTH, q), F32),
            pltpu.VMEM((n_chunks * DT_WIDTH, q), F32),
            pltpu.VMEM((seq, gw), F32),
            pltpu.VMEM((SSM_STATE, gw), F32),
            pltpu.VMEM((SSM_STATE, gw), F32),
        ],
        compiler_params=_params("parallel", "arbitrary"),
        name="ssd",
    )(xc, bc, cc, dt, z, sp["dtb"], sp["alog"], sp["dskip"], sp["nw"])


def _aligned(start, multiple):
    return start if isinstance(start, int) else pl.multiple_of(start, multiple)


def _interleave(*stages):
    live = list(stages)
    while live:
        for st in list(live):
            if next(st, StopIteration) is StopIteration:
                live.remove(st)


def _gqa_kernel(q_ref, k_ref, vt_ref, o_ref, s_a, s_b, *, ctx_len, tq, skip_ctx):
    seq = k_ref.shape[0]
    kc = tq
    slots = (s_a, s_b)
    o_shift = ctx_len if skip_ctx else 0

    def scores(q_start, r, nk, out):
        q = q_ref[pl.ds(q_start, tq), r * GQA_HEAD_DIM:(r + 1) * GQA_HEAD_DIM]
        m = None
        for j in range(nk // kc):
            s = _dot_nt(k_ref[j * kc:(j + 1) * kc, :], q)
            slots[r % 2][j * kc:(j + 1) * kc, :] = s
            mj = jnp.max(s, axis=0, keepdims=True)
            m = mj if m is None else jnp.maximum(m, mj)
            yield
        out.append(m)

    def values(o_start, r, nk, m):
        l = None
        acc = None
        for j in range(nk // kc):
            p = jnp.exp2(slots[r % 2][j * kc:(j + 1) * kc, :] - m)
            lj = jnp.sum(p, axis=0, keepdims=True)
            l = lj if l is None else l + lj
            t = _dot(vt_ref[:, j * kc:(j + 1) * kc], p.astype(BF16))
            acc = t if acc is None else acc + t
            yield
        o_ref[pl.ds(o_start, tq), r * GQA_HEAD_DIM:(r + 1) * GQA_HEAD_DIM] = (
            (acc / l).T.astype(BF16))

    def tile(q_start, nk, m0, next_start):
        m = m0
        for r in range(GQA_GROUP):
            out = []
            stages = [values(_aligned(q_start - o_shift, tq), r, nk, m)]
            if r + 1 < GQA_GROUP:
                stages.append(scores(q_start, r + 1, nk, out))
            elif next_start is not None:
                stages.append(scores(next_start, 0, nk, out))
            _interleave(*stages)
            m = out[0] if out else None
        return m

    def first_scores(q_start, nk):
        out = []
        _interleave(scores(q_start, 0, nk, out))
        return out[0]

    if not skip_ctx:
        n_ctx = ctx_len // tq
        m = first_scores(0, ctx_len)
        for i in range(n_ctx):
            m = tile(i * tq, ctx_len, m, (i + 1) * tq if i + 1 < n_ctx else None)

    n_lat = (seq - ctx_len) // tq
    m = first_scores(ctx_len, seq)

    def body(i, m):
        start = pl.multiple_of(ctx_len + i * tq, tq)
        return tile(start, seq, m, pl.multiple_of(start + tq, tq))

    m = lax.fori_loop(0, n_lat - 1, body, m)
    tile(seq - tq, seq, m, None)


def _gqa_call(q, k, vt, ctx_len, tq, skip_ctx):
    bsz, seq, _ = q.shape
    gwid = GQA_GROUP * GQA_HEAD_DIM
    out_rows = seq - ctx_len if skip_ctx else seq
    kern = functools.partial(_gqa_kernel, ctx_len=ctx_len, tq=tq, skip_ctx=skip_ctx)
    return pl.pallas_call(
        kern,
        grid=(bsz, GQA_KV_HEADS),
        in_specs=[pl.BlockSpec((None, seq, gwid), lambda b, g: (b, 0, g)),
                  pl.BlockSpec((None, seq, GQA_HEAD_DIM), lambda b, g: (b, 0, g)),
                  pl.BlockSpec((None, GQA_HEAD_DIM, seq), lambda b, g: (b, g, 0))],
        out_specs=pl.BlockSpec((None, out_rows, gwid), lambda b, g: (b, 0, g)),
        out_shape=jax.ShapeDtypeStruct((bsz, out_rows, GQA_WIDTH), BF16),
        scratch_shapes=[pltpu.VMEM((seq, tq), F32), pltpu.VMEM((seq, tq), F32)],
        compiler_params=_params("parallel", "parallel"),
        name="gqa",
    )(q, k, vt)


def _diff_kernel(q_ref, k_ref, v_ref, lam_ref, nw_ref, o_ref, s_a, s_b, m_a, m_b, *, ctx_len, tq,
                 skip_ctx, lam_init):
    lv = lam_ref[...]
    lam = (jnp.exp(jnp.sum(lv[0:1] * lv[1:2], axis=-1, keepdims=True))
           - jnp.exp(jnp.sum(lv[2:3] * lv[3:4], axis=-1, keepdims=True)) + lam_init)
    seq = k_ref.shape[0]
    slots = (s_a, s_b)
    maxes = (m_a, m_b)
    o_shift = ctx_len if skip_ctx else 0

    def scores(q_start, nk, slot):
        q = q_ref[pl.ds(q_start, tq), :]
        lane = lax.broadcasted_iota(jnp.int32, q.shape, 1)
        zero = jnp.zeros_like(q)
        q2 = jnp.concatenate([jnp.where(lane < DIFF_HEAD_DIM, q, zero),
                              jnp.where(lane >= DIFF_HEAD_DIM, q, zero)], axis=0)
        s = _dot_nt(q2, k_ref[0:nk, :])
        slots[slot][:, 0:nk] = s
        maxes[slot][...] = jnp.max(s, axis=-1, keepdims=True)

    def finish(nk, slot, o_start):
        s = slots[slot][:, 0:nk]
        p = jnp.exp2(s - maxes[slot][...])
        l = jnp.sum(p, axis=-1, keepdims=True)
        l1, l2 = l[0:tq], l[tq:2 * tq]
        w = p[0:tq] - p[tq:2 * tq] * (lam * l1 / l2)
        o = _dot(w.astype(BF16), v_ref[0:nk, :]) * (1.0 / l1)
        ms = jnp.mean(o * o, axis=-1, keepdims=True)
        o_ref[pl.ds(o_start, tq), :] = (
            o * lax.rsqrt(ms + EPS) * nw_ref[...] * (1.0 - lam_init)).astype(BF16)

    if not skip_ctx:
        for i in range(ctx_len // tq):
            scores(i * tq, ctx_len, 0)
            finish(ctx_len, 0, i * tq)

    def pair(t0, last):
        scores(_aligned(t0 + tq, tq), seq, 1)
        finish(seq, 0, _aligned(t0 - o_shift, tq))
        if not last:
            scores(_aligned(t0 + 2 * tq, tq), seq, 0)
        finish(seq, 1, _aligned(t0 + tq - o_shift, tq))

    n_pairs = (seq - ctx_len) // (2 * tq)
    scores(ctx_len, seq, 0)
    for j in range(n_pairs):
        pair(ctx_len + 2 * j * tq, j == n_pairs - 1)


def _diff_call(q, k, v, lam_p, nw, lam_init, ctx_len, tq, skip_ctx):
    bsz, seq, _ = q.shape
    out_rows = seq - ctx_len if skip_ctx else seq
    kern = functools.partial(_diff_kernel, ctx_len=ctx_len, tq=tq, skip_ctx=skip_ctx,
                             lam_init=lam_init)
    head = pl.BlockSpec((None, seq, LANES), lambda b, h: (b, 0, h))
    return pl.pallas_call(
        kern,
        grid=(bsz, DIFF_HEADS),
        in_specs=[head, head, head, _const_spec(lam_p.shape), _const_spec(nw.shape)],
        out_specs=pl.BlockSpec((None, out_rows, LANES), lambda b, h: (b, 0, h)),
        out_shape=jax.ShapeDtypeStruct((bsz, out_rows, DIFF_WIDTH), BF16),
        scratch_shapes=[pltpu.VMEM((2 * tq, seq), F32), pltpu.VMEM((2 * tq, seq), F32),
                        pltpu.VMEM((2 * tq, 1), F32), pltpu.VMEM((2 * tq, 1), F32)],
        compiler_params=_params("parallel", "parallel"),
        name="diff_attn",
    )(q, k, v, lam_p, nw)


def _merge_kernel(x_ref, mod_ref, ys_ref, yg_ref, yd_ref, wg_ref, bg_ref, ws_ref, wq_ref, wd_ref,
                  wo_ref, lg_ref, lb_ref, o_ref, *, n_ctx_rows):
    d = D_MODEL
    x = x_ref[...]
    h = _modulated(x_ref, mod_ref, 0, n_ctx_rows)
    m = None
    for i, (y_ref, w_ref) in enumerate(((ys_ref, ws_ref), (yg_ref, wq_ref), (yd_ref, wd_ref))):
        gate = _sigmoid(_dot(h, wg_ref[:, i * d:(i + 1) * d]) + bg_ref[:, i * d:(i + 1) * d])
        term = gate * _dot(y_ref[...], w_ref[...])
        m = term if m is None else m + term
    y = _dot(m.astype(BF16), wo_ref[...])
    r = DEEPNORM_ALPHA * x + _mod_row(mod_ref, 2, n_ctx_rows, x.shape[0]) * y
    o_ref[...] = _layer_norm(r, lg_ref[...], lb_ref[...])


def _merge_call(xs, mods, ys, yg, yd, lw, n_ctx_rows, tm, row_off):
    bsz, seq, d = xs.shape
    full = lambda w: pl.BlockSpec((None, tm, w), lambda b, t: (b, t + row_off, 0))
    tok = lambda w: pl.BlockSpec((None, tm, w), lambda b, t: (b, t, 0))
    consts = [lw["w_gate"], lw["b_gate"], lw["w_ssm_out"], lw["w_gqa_out"], lw["w_diff_out"],
              lw["w_o"], lw["ln1_g"], lw["ln1_b"]]
    return pl.pallas_call(
        functools.partial(_merge_kernel, n_ctx_rows=n_ctx_rows),
        grid=(bsz, seq // tm - row_off),
        in_specs=[full(d), _mod_spec(d),
                  full(SSM_D_INNER), tok(GQA_WIDTH), tok(DIFF_WIDTH)]
                 + [_const_spec(a.shape) for a in consts],
        out_specs=tok(d),
        out_shape=jax.ShapeDtypeStruct((bsz, seq - row_off * tm, d), F32),
        compiler_params=_params("parallel", "parallel"),
        name="merge",
    )(xs, mods, ys, yg, yd, *consts)


def _ffn_kernel(x_ref, mod_ref, wi_ref, wo_ref, lg_ref, lb_ref, o_ref, *, n_ctx_rows):
    x = x_ref[...]
    h = _modulated(x_ref, mod_ref, 3, n_ctx_rows)
    acc = None
    for c0 in range(0, FFN_HIDDEN, FFN_CHUNK):
        a = _dot(h, wi_ref[:, c0:c0 + FFN_CHUNK])
        b = _dot(h, wi_ref[:, FFN_HIDDEN + c0:FFN_HIDDEN + c0 + FFN_CHUNK])
        u = (_silu(a) * b).astype(BF16)
        part = _dot(u, wo_ref[c0:c0 + FFN_CHUNK, :])
        acc = part if acc is None else acc + part
    r = DEEPNORM_ALPHA * x + _mod_row(mod_ref, 5, n_ctx_rows, x.shape[0]) * acc
    o_ref[...] = _layer_norm(r, lg_ref[...], lb_ref[...])


def _ffn_call(xs, mods, lw, n_ctx_rows, tm):
    bsz, seq, d = xs.shape
    tok = pl.BlockSpec((None, tm, d), lambda b, t: (b, t, 0))
    consts = [lw["ffn_w_in"], lw["ffn_w_out"], lw["ln2_g"], lw["ln2_b"]]
    return pl.pallas_call(
        functools.partial(_ffn_kernel, n_ctx_rows=n_ctx_rows),
        grid=(bsz, seq // tm),
        in_specs=[tok, _mod_spec(d)]
                 + [_const_spec(a.shape) for a in consts],
        out_specs=tok,
        out_shape=jax.ShapeDtypeStruct((bsz, seq, d), F32),
        compiler_params=_params("parallel", "parallel"),
        name="ffn",
    )(xs, mods, *consts)


def _rope_tables(ctx_len, lat_len, head_dim, half_sign_period):
    t = jnp.arange(lat_len, dtype=jnp.int32)
    d_axis = head_dim // 2
    inv_freq = ROPE_THETA ** (-jnp.arange(0, d_axis, 2, dtype=F32) / d_axis)
    ang_r = (t // GRID_W).astype(F32)[:, None] * inv_freq
    ang_c = (t % GRID_W).astype(F32)[:, None] * inv_freq
    ang = jnp.concatenate([ang_r, ang_r, ang_c, ang_c], axis=-1)
    cos, sin = jnp.cos(ang), jnp.sin(ang)
    reps = LANES // head_dim
    cos, sin = jnp.tile(cos, (1, reps)), jnp.tile(sin, (1, reps))
    lane = jnp.arange(LANES)
    sign = jnp.where((lane % half_sign_period) < half_sign_period // 2, -1.0, 1.0).astype(F32)
    cos = jnp.concatenate([jnp.ones((ctx_len, LANES), F32), cos], axis=0)
    sin = jnp.concatenate([jnp.zeros((ctx_len, LANES), F32), sin * sign], axis=0)
    return cos, sin


def _group_rows(v, width):
    return v.reshape(SSM_GROUPS, 1, width)


def _ssm_params(dt_bias, a_log, d_skip, norm_w):
    gh, ng = SSM_GROUP_HEADS, SSM_GROUPS

    def per_dir(v):
        r = jnp.transpose(v.reshape(2, ng, gh), (1, 0, 2)).reshape(ng, 1, 2 * gh)
        return jnp.pad(r, ((0, 0), (0, 0), (0, DT_WIDTH - 2 * gh)))

    return {
        "dtb": per_dir(dt_bias), "alog": per_dir(a_log),
        "dskip": _group_rows(jnp.repeat(d_skip, SSM_HEAD_DIM), SSM_GROUP_WIDTH),
        "nw": _group_rows(norm_w, SSM_GROUP_WIDTH),
    }


def kernel(x, c, ctx, c_ctx, ada_w, ada_b, w_in, b_gate, ssm_conv_w, ssm_conv_b, ssm_dt_bias,
           ssm_a_log, ssm_d, ssm_norm_w, w_ssm_out, gqa_q_norm, gqa_k_norm, w_gqa_out, diff_lambda,
           diff_norm_w, w_diff_out, w_o, ln1_g, ln1_b, ffn_w_in, ffn_w_out, ln2_g, ln2_b):
    bsz, lat_len, d = x.shape
    ctx_len = ctx.shape[1]
    depth = w_in.shape[0]
    assert d == D_MODEL and depth == DEPTH and w_in.shape[2] == IN_WIDTH
    assert lat_len % GRID_W == 0 and lat_len % SSM_CHUNK == 0 and ctx_len % SSM_CHUNK == 0
    tq = math.gcd(256, ctx_len)
    seq = ctx_len + lat_len
    tm = _row_tile(seq, tq)
    tm_lat = _row_tile(lat_len, tq)
    tm_ssm = _row_tile(seq, SSM_CHUNK, SSM_ROW_TILE)
    assert (lat_len // tq) % 2 == 0

    rows = -(-(bsz + 1) // SUBLANES) * SUBLANES
    cvec = jnp.concatenate([c, c_ctx[None], jnp.zeros((rows - bsz - 1, d), F32)], axis=0)
    ada = _ada_call(cvec, ada_w, ada_b)

    rope_g = _rope_tables(ctx_len, lat_len, GQA_HEAD_DIM, GQA_HEAD_DIM // 2)
    rope_d = _rope_tables(ctx_len, lat_len, DIFF_HEAD_DIM, DIFF_HEAD_DIM // 2)
    tabs = (*rope_g, *rope_d)

    xs = jnp.concatenate([ctx, x], axis=1)
    pad = jnp.zeros((d, DT_WIDTH - 2 * SSM_HEADS), BF16)
    for i in range(depth):
        need_ctx = i < depth - 1
        lam_init = 0.8 - 0.6 * math.exp(-0.3 * i)
        mod_l = ada[i, :bsz].reshape(bsz, 1, 6, d)
        mod_c = jnp.broadcast_to(ada[i, bsz].reshape(1, 1, 6, d), (bsz, 1, 6, d))
        mods = jnp.concatenate([mod_c, mod_l], axis=1)
        w = w_in[i].astype(BF16)
        w_ssm = jnp.concatenate([w[:, :OFF_GQ], pad], axis=1)
        w_attn = w[:, OFF_GQ:OFF_GATE]
        lw = {
            "w_gate": w[:, OFF_GATE:], "b_gate": b_gate[i][None],
            "w_ssm_out": w_ssm_out[i].astype(BF16), "w_gqa_out": w_gqa_out[i].astype(BF16),
            "w_diff_out": w_diff_out[i].astype(BF16), "w_o": w_o[i].astype(BF16),
            "ln1_g": ln1_g[i][None], "ln1_b": ln1_b[i][None],
            "ffn_w_in": ffn_w_in[i].astype(BF16), "ffn_w_out": ffn_w_out[i].astype(BF16),
            "ln2_g": ln2_g[i][None], "ln2_b": ln2_b[i][None],
        }
        sp = _ssm_params(ssm_dt_bias[i], ssm_a_log[i], ssm_d[i], ssm_norm_w[i])
        conv_wb = jnp.concatenate([ssm_conv_w[i], ssm_conv_b[i][None],
                                   jnp.zeros((SUBLANES - SSM_CONV - 1, SSM_CONV_DIM), F32)], axis=0)

        z, xc, bc, cc, dt = _inproj_ssm_call(xs, mods, w_ssm, conv_wb, ctx_len, tm_ssm)
        gq, gk, gvt, dq, dk, dv = _inproj_attn_call(
            xs, mods, w_attn, gqa_q_norm[i][None], gqa_k_norm[i][None], tabs, ctx_len, tm)
        y_ssm = _ssd_call(xc, bc, cc, dt, z, sp, ctx_len)
        y_gqa = _gqa_call(gq, gk, gvt, ctx_len, tq, not need_ctx)
        y_diff = _diff_call(dq, dk, dv, diff_lambda[i], diff_norm_w[i][None], lam_init, ctx_len, tq,
                            not need_ctx)
        if need_ctx:
            x1 = _merge_call(xs, mods, y_ssm, y_gqa, y_diff, lw, ctx_len, tm, 0)
            xs = _ffn_call(x1, mods, lw, ctx_len, tm)
        else:
            x1 = _merge_call(xs, mods, y_ssm, y_gqa, y_diff, lw, 0, tq, ctx_len // tq)
            xs = _ffn_call(x1, mods, lw, 0, tm_lat)
    return xs
```

```python
import functools
import math

import jax
import jax.numpy as jnp
from jax import lax
from jax.experimental import pallas as pl
from jax.experimental.pallas import tpu as pltpu

F32 = jnp.float32
BF16 = jnp.bfloat16

D_MODEL = 1024
DEPTH = 4
GRID_W = 64
ROPE_THETA = 10000.0
EPS = 1e-6

SSM_D_INNER = 2048
SSM_HEAD_DIM = 64
SSM_HEADS = 32
SSM_GROUPS = 4
SSM_STATE = 128
SSM_CONV = 5
SSM_CHUNK = 128
SSM_BC = SSM_GROUPS * SSM_STATE
SSM_CONV_DIM = SSM_D_INNER + 2 * SSM_BC
SSM_GROUP_WIDTH = SSM_D_INNER // SSM_GROUPS
SSM_GROUP_HEADS = SSM_HEADS // SSM_GROUPS

GQA_HEAD_DIM = 128
GQA_HEADS = 8
GQA_KV_HEADS = 2
GQA_GROUP = GQA_HEADS // GQA_KV_HEADS
GQA_WIDTH = GQA_HEADS * GQA_HEAD_DIM
GQA_KV_WIDTH = GQA_KV_HEADS * GQA_HEAD_DIM

DIFF_HEAD_DIM = 64
DIFF_HEADS = 8
DIFF_WIDTH = DIFF_HEADS * 2 * DIFF_HEAD_DIM

N_BRANCHES = 3
FFN_HIDDEN = 2816
FFN_CHUNK = 256

DEEPNORM_ALPHA = (2 * DEPTH) ** 0.25

LANES = 128
SUBLANES = 8
DT_WIDTH = LANES
SSM_CONV_TILES = SSM_CONV_DIM // LANES
LOG2E = 1.4426950408889634
VMEM_LIMIT = 56 * 1024 * 1024
MAX_ROW_TILE = 768
CONV_HALO = 16
SSM_ROW_TILE = 384

OFF_Z = 0
OFF_XBC = OFF_Z + SSM_D_INNER
OFF_DT = OFF_XBC + SSM_CONV_DIM
OFF_GQ = OFF_DT + 2 * SSM_HEADS
OFF_GK = OFF_GQ + GQA_WIDTH
OFF_GV = OFF_GK + GQA_KV_WIDTH
OFF_DQ = OFF_GV + GQA_KV_WIDTH
OFF_DK = OFF_DQ + DIFF_WIDTH
OFF_DV = OFF_DK + DIFF_WIDTH
OFF_GATE = OFF_DV + DIFF_WIDTH
IN_WIDTH = OFF_GATE + N_BRANCHES * D_MODEL


def _dot(a, b):
    return jnp.dot(a, b, preferred_element_type=F32)


def _dot_nt(a, b):
    return lax.dot_general(a, b, (((1,), (1,)), ((), ())), preferred_element_type=F32)


def _split3(a):
    hi = a.astype(BF16)
    r = a - hi.astype(F32)
    mid = r.astype(BF16)
    lo = (r - mid.astype(F32)).astype(BF16)
    return hi, mid, lo


def _dot_exact_rhs(a, sel):
    hi, mid, lo = _split3(a)
    return _dot(hi, sel) + _dot(mid, sel) + _dot(lo, sel)


def _sigmoid(x):
    return 1.0 / (1.0 + jnp.exp(-x))


def _silu(x):
    return x * _sigmoid(x)


def _layer_norm(r, g, b):
    mu = jnp.mean(r, axis=-1, keepdims=True)
    d = r - mu
    var = jnp.mean(d * d, axis=-1, keepdims=True)
    return d * lax.rsqrt(var + EPS) * g + b


def _params(*sem):
    return pltpu.CompilerParams(dimension_semantics=sem, vmem_limit_bytes=VMEM_LIMIT)


def _const_spec(shape):
    nd = len(shape)
    return pl.BlockSpec(shape, lambda *_: (0,) * nd, pipeline_mode=pl.Buffered(1))


def _ada_kernel(c_ref, w_ref, b_ref, o_ref):
    sc = _silu(c_ref[...]).astype(BF16)
    o_ref[...] = _dot(sc, w_ref[...].astype(BF16)) + b_ref[...]


def _ada_call(cvec, ada_w, ada_b):
    rows = cvec.shape[0]
    depth, d, n = ada_w.shape
    tn = n // 4
    return pl.pallas_call(
        _ada_kernel,
        grid=(depth, n // tn),
        in_specs=[
            pl.BlockSpec((rows, d), lambda i, j: (0, 0)),
            pl.BlockSpec((None, d, tn), lambda i, j: (i, 0, j)),
            pl.BlockSpec((None, 1, tn), lambda i, j: (i, 0, j)),
        ],
        out_specs=pl.BlockSpec((None, rows, tn), lambda i, j: (i, 0, j)),
        out_shape=jax.ShapeDtypeStruct((depth, rows, n), F32),
        compiler_params=_params("arbitrary", "arbitrary"),
        name="ada",
    )(cvec, ada_w, ada_b.reshape(depth, 1, n))


def _mod_row(mod_ref, row, n_ctx_rows, n, first=None):
    lat = mod_ref[1, row:row + 1, :]
    if n_ctx_rows == 0:
        return lat
    if first is None:
        first = pl.program_id(1) * n
    is_ctx = first + lax.broadcasted_iota(jnp.int32, (n, 1), 0) < n_ctx_rows
    return jnp.where(is_ctx, mod_ref[0, row:row + 1, :], lat)


def _modulated(x_ref, mod_ref, shift_row, n_ctx_rows, first=None):
    x = x_ref[...]
    shift = _mod_row(mod_ref, shift_row, n_ctx_rows, x.shape[0], first)
    scale = _mod_row(mod_ref, shift_row + 1, n_ctx_rows, x.shape[0], first)
    return (x * (1.0 + scale) + shift).astype(BF16)


def _inproj_ssm_kernel(x_ref, xp_ref, xn_ref, mod_ref, w_ref, cw_ref, z_ref, xc_ref, bc_ref, cc_ref,
                       dt_ref, u_scr, *, n_ctx_rows, seq):
    tm = x_ref.shape[0]
    halo = CONV_HALO
    first = pl.program_id(1) * tm
    h = _modulated(x_ref, mod_ref, 0, n_ctx_rows)
    h_all = jnp.concatenate([_modulated(xp_ref, mod_ref, 0, n_ctx_rows, first - halo), h,
                             _modulated(xn_ref, mod_ref, 0, n_ctx_rows, first + tm)], axis=0)
    step = 512

    prev_ok = jnp.logical_and(first != 0, first != n_ctx_rows)
    next_ok = jnp.logical_and(first + tm != seq, first + tm != n_ctx_rows)
    split = n_ctx_rows % tm
    center = SSM_CONV // 2

    def conv(j, lo, n, valid=None):
        cw = cw_ref[:, j * LANES:(j + 1) * LANES]
        acc = None
        for k in range(SSM_CONV):
            win = u_scr[j, halo + lo + k - center:halo + lo + k - center + n, :]
            if valid is not None and k != center:
                win = jnp.where(valid[k], win, 0.0)
            term = win * cw[k:k + 1, :]
            acc = term if acc is None else acc + term
        return _silu(acc + cw[SSM_CONV:SSM_CONV + 1, :])

    def dest(j):
        c = j * LANES
        if c < SSM_D_INNER:
            return xc_ref, slice(c, c + LANES)
        if c < SSM_D_INNER + SSM_BC:
            return bc_ref, slice(c - SSM_D_INNER, c - SSM_D_INNER + LANES)
        return cc_ref, slice(c - SSM_D_INNER - SSM_BC, c - SSM_D_INNER - SSM_BC + LANES)

    per = step // LANES
    for ci in range(SSM_CONV_DIM // step):
        c0 = OFF_XBC + ci * step
        u = _dot(h_all, w_ref[:, c0:c0 + step])
        for j in range(per):
            cols = slice(j * LANES, (j + 1) * LANES)
            t = ci * per + j
            u_scr[t, 0:halo, :] = jnp.where(prev_ok, u[0:halo, cols], 0.0)
            u_scr[t, halo:halo + tm, :] = u[halo:halo + tm, cols]
            u_scr[t, halo + tm:halo + tm + halo, :] = jnp.where(next_ok, u[halo + tm:, cols], 0.0)
        for j in range(per):
            ref, cols = dest(ci * per + j)
            ref[:, cols] = conv(ci * per + j, 0, tm)
    for c0 in range(0, SSM_D_INNER, step):
        z_ref[:, c0:c0 + step] = _silu(_dot(h, w_ref[:, OFF_Z + c0:OFF_Z + c0 + step]))
    dt_ref[...] = _dot(h, w_ref[:, OFF_DT:OFF_DT + DT_WIDTH])
    if split:
        @pl.when(pl.program_id(1) == n_ctx_rows // tm)
        def _():
            lo, n = split - SUBLANES, 2 * SUBLANES
            r = lo + lax.broadcasted_iota(jnp.int32, (n, LANES), 0)
            valid = [(r < split) == (r + k - center < split) for k in range(SSM_CONV)]
            for t in range(SSM_CONV_TILES):
                ref, cols = dest(t)
                ref[lo:lo + n, cols] = conv(t, lo, n, valid)


def _mod_spec(d):
    return pl.BlockSpec((None, 2, 6, d), lambda b, t: (b, 0, 0, 0))


def _row_tile(rows, unit, limit=None):
    best = unit
    for k in range(1, (limit or MAX_ROW_TILE) // unit + 1):
        if rows % (k * unit) == 0:
            best = k * unit
    return best


def _inproj_ssm_call(xs, mods, w_ssm, conv_wb, ctx_len, tm):
    bsz, seq, d = xs.shape
    halo = CONV_HALO
    per = tm // halo
    last = seq // halo - 1
    tok = lambda w: pl.BlockSpec((None, tm, w), lambda b, t: (b, t, 0))
    out = lambda w: jax.ShapeDtypeStruct((bsz, seq, w), F32)
    return pl.pallas_call(
        functools.partial(_inproj_ssm_kernel, n_ctx_rows=ctx_len, seq=seq),
        grid=(bsz, seq // tm),
        in_specs=[tok(d),
                  pl.BlockSpec((None, halo, d), lambda b, t: (b, jnp.maximum(t * per - 1, 0), 0)),
                  pl.BlockSpec((None, halo, d), lambda b, t: (b, jnp.minimum((t + 1) * per, last), 0)),
                  _mod_spec(d), _const_spec(w_ssm.shape), _const_spec(conv_wb.shape)],
        out_specs=[tok(SSM_D_INNER), tok(SSM_D_INNER), tok(SSM_BC), tok(SSM_BC), tok(DT_WIDTH)],
        out_shape=[out(SSM_D_INNER), out(SSM_D_INNER), out(SSM_BC), out(SSM_BC), out(DT_WIDTH)],
        scratch_shapes=[pltpu.VMEM((SSM_CONV_TILES, tm + 2 * halo, LANES), F32)],
        compiler_params=_params("parallel", "parallel"),
        name="inproj_ssm",
    )(xs, xs, xs, mods, w_ssm, conv_wb)


def _rope(u, cos, sin_signed, half):
    lane = lax.broadcasted_iota(jnp.int32, u.shape, 1)
    fwd = pltpu.roll(u, LANES - half, axis=1)
    bwd = pltpu.roll(u, half, axis=1)
    partner = jnp.where((lane % (2 * half)) < half, fwd, bwd)
    return u * cos + partner * sin_signed


def _rms_heads(u, g):
    ms = jnp.mean(u * u, axis=-1, keepdims=True)
    return u * lax.rsqrt(ms + EPS) * g


def _inproj_attn_kernel(x_ref, mod_ref, w_ref, qn_ref, kn_ref, cg_ref, sg_ref, cd_ref, sd_ref,
                        gq_ref, gk_ref, gv_ref, dq_ref, dk_ref, dv_ref, *, n_ctx_rows):
    h = _modulated(x_ref, mod_ref, 0, n_ctx_rows)
    base = OFF_GQ
    cg, sg = cg_ref[...], sg_ref[...]
    cd, sd = cd_ref[...], sd_ref[...]
    hd = GQA_HEAD_DIM

    def heads(off, width):
        step = min(512, width)
        for c0 in range(0, width, step):
            u = _dot(h, w_ref[:, off - base + c0:off - base + c0 + step])
            for j in range(step // LANES):
                yield (c0 // LANES + j) * LANES, u[:, j * LANES:(j + 1) * LANES]

    gqa_scale = GQA_HEAD_DIM ** -0.5 * LOG2E
    for c, u in heads(OFF_GQ, GQA_WIDTH):
        u = _rope(_rms_heads(u, qn_ref[...]), cg, sg, GQA_HEAD_DIM // 4)
        gq_ref[:, c:c + hd] = (u * gqa_scale).astype(BF16)
    for c, u in heads(OFF_GK, GQA_KV_WIDTH):
        u = _rope(_rms_heads(u, kn_ref[...]), cg, sg, GQA_HEAD_DIM // 4)
        gk_ref[:, c:c + hd] = u.astype(BF16)
    for c, u in heads(OFF_GV, GQA_KV_WIDTH):
        gv_ref[c:c + hd, :] = u.T.astype(BF16)
    scale = DIFF_HEAD_DIM ** -0.5 * LOG2E
    for c, u in heads(OFF_DQ, DIFF_WIDTH):
        dq_ref[:, c:c + LANES] = (_rope(u, cd, sd, DIFF_HEAD_DIM // 4) * scale).astype(BF16)
    for c, u in heads(OFF_DK, DIFF_WIDTH):
        dk_ref[:, c:c + LANES] = _rope(u, cd, sd, DIFF_HEAD_DIM // 4).astype(BF16)
    for c, u in heads(OFF_DV, DIFF_WIDTH):
        dv_ref[c:c + LANES, :] = u.T.astype(BF16)


def _inproj_attn_call(xs, mods, w_attn, qn, kn, tabs, ctx_len, tm):
    bsz, seq, d = xs.shape
    tok = lambda w: pl.BlockSpec((None, tm, w), lambda b, t: (b, t, 0))
    tab = pl.BlockSpec((tm, LANES), lambda b, t: (t, 0))
    tr = lambda w: pl.BlockSpec((None, w, tm), lambda b, t: (b, 0, t))
    row = lambda w: jax.ShapeDtypeStruct((bsz, seq, w), BF16)
    col = lambda w: jax.ShapeDtypeStruct((bsz, w, seq), BF16)
    return pl.pallas_call(
        functools.partial(_inproj_attn_kernel, n_ctx_rows=ctx_len),
        grid=(bsz, seq // tm),
        in_specs=[tok(d), _mod_spec(d), _const_spec(w_attn.shape),
                  _const_spec(qn.shape), _const_spec(kn.shape), tab, tab, tab, tab],
        out_specs=[tok(GQA_WIDTH), tok(GQA_KV_WIDTH), tr(GQA_KV_WIDTH),
                   tok(DIFF_WIDTH), tok(DIFF_WIDTH), tr(DIFF_WIDTH)],
        out_shape=[row(GQA_WIDTH), row(GQA_KV_WIDTH), col(GQA_KV_WIDTH),
                   row(DIFF_WIDTH), row(DIFF_WIDTH), col(DIFF_WIDTH)],
        compiler_params=_params("parallel", "parallel"),
        name="inproj_attn",
    )(xs, mods, w_attn, qn, kn, *tabs)


def _ssd_kernel(xs_s, b_ref, c_s, dtr_ref, z_ref, dtb_ref, alog_ref, dskip_ref, nw_ref, o_ref,
                bt_s, cb_s, acs_s, acst_s, dtt_s, y_s, stf_s, stb_s, *, n_chunks, n_ctx_chunks):
    q = SSM_CHUNK
    gh = SSM_GROUP_HEADS
    hp = SSM_HEAD_DIM
    g = pl.program_id(1)

    row_i = lax.broadcasted_iota(jnp.int32, (q, q), 0)
    col_i = lax.broadcasted_iota(jnp.int32, (q, q), 1)
    tri_lo = (col_i <= row_i).astype(BF16)
    tri_up = (col_i >= row_i).astype(BF16)
    src = jnp.where(col_i < gh, g * gh + col_i, SSM_HEADS + g * gh + col_i - gh)
    pick = ((row_i == src) & (col_i < 2 * gh)).astype(BF16)
    a_neg = -jnp.exp(alog_ref[...])
    dt_bias = dtb_ref[...]

    seq = n_chunks * q
    raw = _dot_exact_rhs(dtr_ref[...], pick) + dt_bias
    dtv = jnp.maximum(raw, 0.0) + jnp.log1p(jnp.exp(-jnp.abs(raw)))
    a_cols = jnp.broadcast_to(a_neg, (q, q)).T
    for c in range(n_chunks):
        rows = slice(c * q, (c + 1) * q)
        dt_t = dtv[rows, :].T
        dtt_s[rows, :] = dt_t
        acst_s[rows, :] = dt_t * a_cols
        bc = b_ref[rows, :]
        bt_s[rows, :] = bc.T
        cb_s[rows, :] = _dot_nt(c_s[rows, :].astype(BF16), bc.astype(BF16))
    hi, mid, lo = _split3(acst_s[...])
    prefix = _dot(hi, tri_up) + _dot(mid, tri_up) + _dot(lo, tri_up)
    suffix = _dot(hi, tri_lo) + _dot(mid, tri_lo) + _dot(lo, tri_lo)
    head_col = lax.broadcasted_iota(jnp.int32, (seq, q), 0) & (q - 1)
    acst_all = jnp.where(head_col < gh, prefix, suffix)
    acst_s[...] = acst_all
    for c in range(n_chunks):
        rows = slice(c * q, (c + 1) * q)
        acs_s[rows, :] = acst_all[rows, :].T

    low_half = col_i < hp

    def scan_chunk(c, direction):
        st_s = (stf_s, stb_s)[direction]
        r0 = _aligned(c * q, q)
        rows = pl.ds(r0, q)
        xs = xs_s[rows, :]
        xb = xs.astype(BF16)
        cc = c_s[rows, :]
        acs = acs_s[rows, :]
        acst = acst_s[rows, :]
        dtt = dtt_s[rows, :]
        btf = bt_s[rows, :]
        cb = cb_s[rows, :]
        end = q - 1 if direction == 0 else 0
        tot = acst[:, end:end + 1]
        mine = (row_i >= gh * direction) & (row_i < gh * (direction + 1))
        w_out = dtt * jnp.exp(jnp.where(mine, tot - acst, 0.0))
        e_tot = jnp.exp(tot)
        keep = (row_i >= col_i) if direction == 0 else (col_i >= row_i)
        acs2 = acs * LOG2E
        src2 = (acst - jnp.log(dtt)) * LOG2E
        outs = []
        for pair in range(gh // 2):
            ms, ces, bws, ets = [], [], [], []
            for hh in range(2):
                col = gh * direction + 2 * pair + hh
                a_l = jnp.broadcast_to(acs2[:, col:col + 1], (q, q))
                decay = jnp.where(keep, jnp.exp2(a_l - src2[col:col + 1, :]), 0.0)
                ms.append((cb * decay).astype(BF16))
                ces.append((cc * jnp.exp2(a_l)).astype(BF16))
                bws.append((btf * w_out[col:col + 1, :]).astype(BF16))
                ets.append(e_tot[col:col + 1, :])
            slab = slice(pair * LANES, (pair + 1) * LANES)
            xp = xb[:, slab]
            zero = jnp.zeros_like(xp)
            x_bd = jnp.concatenate([jnp.where(low_half, xp, zero),
                                    jnp.where(low_half, zero, xp)], axis=0)
            st = st_s[:, slab]
            sb = st.astype(BF16)
            s_bd = jnp.concatenate([jnp.where(low_half, sb, zero),
                                    jnp.where(low_half, zero, sb)], axis=0)
            outs.append(_dot(jnp.concatenate(ms + ces, axis=1),
                             jnp.concatenate([x_bd, s_bd], axis=0)))
            grow = _dot(jnp.concatenate(bws, axis=1), x_bd)
            st_s[:, slab] = st * jnp.where(low_half, ets[0], ets[1]) + grow
        return rows, xs, jnp.concatenate(outs, axis=1)

    def finalize(rows, xs, y_scan):
        y = y_scan + dskip_ref[...] * xs
        y = y * z_ref[rows, :]
        ms = jnp.mean(y * y, axis=-1, keepdims=True)
        o_ref[rows, :] = (y * lax.rsqrt(ms + EPS) * nw_ref[...]).astype(BF16)

    stf_s[...] = jnp.zeros_like(stf_s)
    stb_s[...] = jnp.zeros_like(stb_s)

    def step(lo, hi, i, phase):
        rows_f, xs_f, yf = scan_chunk(lo + i, 0)
        if phase == "middle":
            _, _, yb = scan_chunk(lo + i, 1)
            finalize(rows_f, xs_f, yf + yb)
            return
        rows_b, xs_b, yb = scan_chunk(hi - 1 - i, 1)
        if phase == "park":
            y_s[rows_f, :] = yf
            y_s[rows_b, :] = yb
        else:
            finalize(rows_f, xs_f, yf + y_s[rows_f, :])
            finalize(rows_b, xs_b, y_s[rows_b, :] + yb)

    def segment(lo, hi):
        n = hi - lo
        half = n // 2
        if n <= 2:
            for i in range(n):
                step(lo, hi, i, "park" if i < half else ("middle" if 2 * i + 1 == n else "finish"))
            return

        def park(i, carry):
            step(lo, hi, i, "park")
            return carry

        def finish(i, carry):
            step(lo, hi, i, "finish")
            return carry

        lax.fori_loop(0, half, park, 0, unroll=4)
        if n % 2:
            step(lo, hi, half, "middle")
        lax.fori_loop(n - half, n, finish, 0, unroll=2)

    segment(0, n_ctx_chunks)
    segment(n_ctx_chunks, n_chunks)


def _ssd_call(xc, bc, cc, dt, z, sp, ctx_len):
    bsz, seq, _ = xc.shape
    q, gw, ng = SSM_CHUNK, SSM_GROUP_WIDTH, SSM_GROUPS
    n_chunks = seq // q
    grp = lambda r, w: pl.BlockSpec((None, r, w), lambda b, g: (g, 0, 0))
    col = lambda w: pl.BlockSpec((None, seq, w), lambda b, g: (b, 0, g))
    kern = functools.partial(_ssd_kernel, n_chunks=n_chunks, n_ctx_chunks=ctx_len // q)
    return pl.pallas_call(
        kern,
        grid=(bsz, ng),
        in_specs=[
            col(gw), col(SSM_STATE), col(SSM_STATE),
            pl.BlockSpec((None, seq, DT_WIDTH), lambda b, g: (b, 0, 0)),
            col(gw),
            grp(1, DT_WIDTH), grp(1, DT_WIDTH), grp(1, gw), grp(1, gw),
        ],
        out_specs=col(gw),
        out_shape=jax.ShapeDtypeStruct((bsz, seq, SSM_D_INNER), BF16),
        scratch_shapes=[
            pltpu.VMEM((n_chunks * SSM_STATE, q), F32),
            pltpu.VMEM((n_chunks * q, q), F32),
            pltpu.VMEM((seq, DT_WIDTH), F32),
            pltpu.VMEM((n_chunks * DT_WIDTH, q), F32),
            pltpu.VMEM((n_chunks * DT_WIDTH, q), F32),
            pltpu.VMEM((seq, gw), F32),
            pltpu.VMEM((SSM_STATE, gw), F32),
            pltpu.VMEM((SSM_STATE, gw), F32),
        ],
        compiler_params=_params("parallel", "arbitrary"),
        name="ssd",
    )(xc, bc, cc, dt, z, sp["dtb"], sp["alog"], sp["dskip"], sp["nw"])


def _aligned(start, multiple):
    return start if isinstance(start, int) else pl.multiple_of(start, multiple)


def _interleave(*stages):
    live = list(stages)
    while live:
        for st in list(live):
            if next(st, StopIteration) is StopIteration:
                live.remove(st)


def _gqa_kernel(q_ref, k_ref, vt_ref, o_ref, s_a, s_b, *, ctx_len, tq, skip_ctx):
    seq = k_ref.shape[0]
    kc = tq
    slots = (s_a, s_b)
    o_shift = ctx_len if skip_ctx else 0

    def scores(q_start, r, nk, out):
        q = q_ref[pl.ds(q_start, tq), r * GQA_HEAD_DIM:(r + 1) * GQA_HEAD_DIM]
        m = None
        for j in range(nk // kc):
            s = _dot_nt(k_ref[j * kc:(j + 1) * kc, :], q)
            slots[r % 2][j * kc:(j + 1) * kc, :] = s
            mj = jnp.max(s, axis=0, keepdims=True)
            m = mj if m is None else jnp.maximum(m, mj)
            yield
        out.append(m)

    def values(o_start, r, nk, m):
        l = None
        acc = None
        for j in range(nk // kc):
            p = jnp.exp2(slots[r % 2][j * kc:(j + 1) * kc, :] - m)
            lj = jnp.sum(p, axis=0, keepdims=True)
            l = lj if l is None else l + lj
            t = _dot(vt_ref[:, j * kc:(j + 1) * kc], p.astype(BF16))
            acc = t if acc is None else acc + t
            yield
        o_ref[pl.ds(o_start, tq), r * GQA_HEAD_DIM:(r + 1) * GQA_HEAD_DIM] = (
            (acc / l).T.astype(BF16))

    def tile(q_start, nk, m0, next_start):
        m = m0
        for r in range(GQA_GROUP):
            out = []
            stages = [values(_aligned(q_start - o_shift, tq), r, nk, m)]
            if r + 1 < GQA_GROUP:
                stages.append(scores(q_start, r + 1, nk, out))
            elif next_start is not None:
                stages.append(scores(next_start, 0, nk, out))
            _interleave(*stages)
            m = out[0] if out else None
        return m

    def first_scores(q_start, nk):
        out = []
        _interleave(scores(q_start, 0, nk, out))
        return out[0]

    if not skip_ctx:
        n_ctx = ctx_len // tq
        m = first_scores(0, ctx_len)
        for i in range(n_ctx):
            m = tile(i * tq, ctx_len, m, (i + 1) * tq if i + 1 < n_ctx else None)

    n_lat = (seq - ctx_len) // tq
    m = first_scores(ctx_len, seq)

    def body(i, m):
        start = pl.multiple_of(ctx_len + i * tq, tq)
        return tile(start, seq, m, pl.multiple_of(start + tq, tq))

    m = lax.fori_loop(0, n_lat - 1, body, m)
    tile(seq - tq, seq, m, None)


def _gqa_call(q, k, vt, ctx_len, tq, skip_ctx):
    bsz, seq, _ = q.shape
    gwid = GQA_GROUP * GQA_HEAD_DIM
    out_rows = seq - ctx_len if skip_ctx else seq
    kern = functools.partial(_gqa_kernel, ctx_len=ctx_len, tq=tq, skip_ctx=skip_ctx)
    return pl.pallas_call(
        kern,
        grid=(bsz, GQA_KV_HEADS),
        in_specs=[pl.BlockSpec((None, seq, gwid), lambda b, g: (b, 0, g)),
                  pl.BlockSpec((None, seq, GQA_HEAD_DIM), lambda b, g: (b, 0, g)),
                  pl.BlockSpec((None, GQA_HEAD_DIM, seq), lambda b, g: (b, g, 0))],
        out_specs=pl.BlockSpec((None, out_rows, gwid), lambda b, g: (b, 0, g)),
        out_shape=jax.ShapeDtypeStruct((bsz, out_rows, GQA_WIDTH), BF16),
        scratch_shapes=[pltpu.VMEM((seq, tq), F32), pltpu.VMEM((seq, tq), F32)],
        compiler_params=_params("parallel", "parallel"),
        name="gqa",
    )(q, k, vt)


def _diff_kernel(q_ref, k_ref, vt_ref, lam_ref, nw_ref, o_ref, s_a, s_b, *, ctx_len, tq, skip_ctx,
                 lam_init):
    lv = lam_ref[...]
    lam = (jnp.exp(jnp.sum(lv[0:1] * lv[1:2], axis=-1, keepdims=True))
           - jnp.exp(jnp.sum(lv[2:3] * lv[3:4], axis=-1, keepdims=True)) + lam_init)
    seq = k_ref.shape[0]
    kc = tq
    slots = (s_a, s_b)
    o_shift = ctx_len if skip_ctx else 0

    def scores(q_start, i, nk, out):
        q = q_ref[pl.ds(q_start, tq), :]
        lane = lax.broadcasted_iota(jnp.int32, q.shape, 1)
        mine = (lane < DIFF_HEAD_DIM) if i == 0 else (lane >= DIFF_HEAD_DIM)
        q = jnp.where(mine, q, jnp.zeros_like(q))
        m = None
        for j in range(nk // kc):
            s = _dot_nt(k_ref[j * kc:(j + 1) * kc, :], q)
            slots[i][j * kc:(j + 1) * kc, :] = s
            mj = jnp.max(s, axis=0, keepdims=True)
            m = mj if m is None else jnp.maximum(m, mj)
            yield
        out.append(m)

    def values(i, nk, m, res):
        l = None
        acc = None
        for j in range(nk // kc):
            p = jnp.exp2(slots[i][j * kc:(j + 1) * kc, :] - m)
            lj = jnp.sum(p, axis=0, keepdims=True)
            l = lj if l is None else l + lj
            t = _dot(vt_ref[:, j * kc:(j + 1) * kc], p.astype(BF16))
            acc = t if acc is None else acc + t
            yield
        res.append((acc, l))

    def tile(q_start, nk, m0, next_start):
        m = m0
        res = []
        for i in range(2):
            out = []
            stages = [values(i, nk, m, res)]
            if i == 0:
                stages.append(scores(q_start, 1, nk, out))
            elif next_start is not None:
                stages.append(scores(next_start, 0, nk, out))
            _interleave(*stages)
            m = out[0] if out else None
        (a1, l1), (a2, l2) = res
        o = (a1 * (1.0 / l1) - a2 * (lam / l2)).T
        ms = jnp.mean(o * o, axis=-1, keepdims=True)
        o_ref[pl.ds(_aligned(q_start - o_shift, tq), tq), :] = (
            o * lax.rsqrt(ms + EPS) * nw_ref[...] * (1.0 - lam_init)).astype(BF16)
        return m

    def first_scores(q_start, nk):
        out = []
        _interleave(scores(q_start, 0, nk, out))
        return out[0]

    if not skip_ctx:
        n_ctx = ctx_len // tq
        m = first_scores(0, ctx_len)
        for i in range(n_ctx):
            m = tile(i * tq, ctx_len, m, (i + 1) * tq if i + 1 < n_ctx else None)

    n_lat = (seq - ctx_len) // tq
    m = first_scores(ctx_len, seq)

    for i in range(n_lat):
        start = ctx_len + i * tq
        m = tile(start, seq, m, start + tq if i + 1 < n_lat else None)


def _diff_call(q, k, vt, lam_p, nw, lam_init, ctx_len, tq, skip_ctx):
    bsz, seq, _ = q.shape
    out_rows = seq - ctx_len if skip_ctx else seq
    kern = functools.partial(_diff_kernel, ctx_len=ctx_len, tq=tq, skip_ctx=skip_ctx,
                             lam_init=lam_init)
    head = pl.BlockSpec((None, seq, LANES), lambda b, h: (b, 0, h))
    return pl.pallas_call(
        kern,
        grid=(bsz, DIFF_HEADS),
        in_specs=[head, head, pl.BlockSpec((None, LANES, seq), lambda b, h: (b, h, 0)),
                  _const_spec(lam_p.shape), _const_spec(nw.shape)],
        out_specs=pl.BlockSpec((None, out_rows, LANES), lambda b, h: (b, 0, h)),
        out_shape=jax.ShapeDtypeStruct((bsz, out_rows, DIFF_WIDTH), BF16),
        scratch_shapes=[pltpu.VMEM((seq, tq), F32), pltpu.VMEM((seq, tq), F32)],
        compiler_params=_params("parallel", "parallel"),
        name="diff_attn",
    )(q, k, vt, lam_p, nw)


def _merge_kernel(x_ref, mod_ref, ys_ref, yg_ref, yd_ref, wg_ref, bg_ref, ws_ref, wq_ref, wd_ref,
                  wo_ref, lg_ref, lb_ref, o_ref, *, n_ctx_rows):
    d = D_MODEL
    x = x_ref[...]
    h = _modulated(x_ref, mod_ref, 0, n_ctx_rows)
    m = None
    for i, (y_ref, w_ref) in enumerate(((ys_ref, ws_ref), (yg_ref, wq_ref), (yd_ref, wd_ref))):
        gate = _sigmoid(_dot(h, wg_ref[:, i * d:(i + 1) * d]) + bg_ref[:, i * d:(i + 1) * d])
        term = gate * _dot(y_ref[...], w_ref[...])
        m = term if m is None else m + term
    y = _dot(m.astype(BF16), wo_ref[...])
    r = DEEPNORM_ALPHA * x + _mod_row(mod_ref, 2, n_ctx_rows, x.shape[0]) * y
    o_ref[...] = _layer_norm(r, lg_ref[...], lb_ref[...])


def _merge_call(xs, mods, ys, yg, yd, lw, n_ctx_rows, tm, row_off):
    bsz, seq, d = xs.shape
    full = lambda w: pl.BlockSpec((None, tm, w), lambda b, t: (b, t + row_off, 0))
    tok = lambda w: pl.BlockSpec((None, tm, w), lambda b, t: (b, t, 0))
    consts = [lw["w_gate"], lw["b_gate"], lw["w_ssm_out"], lw["w_gqa_out"], lw["w_diff_out"],
              lw["w_o"], lw["ln1_g"], lw["ln1_b"]]
    return pl.pallas_call(
        functools.partial(_merge_kernel, n_ctx_rows=n_ctx_rows),
        grid=(bsz, seq // tm - row_off),
        in_specs=[full(d), _mod_spec(d),
                  full(SSM_D_INNER), tok(GQA_WIDTH), tok(DIFF_WIDTH)]
                 + [_const_spec(a.shape) for a in consts],
        out_specs=tok(d),
        out_shape=jax.ShapeDtypeStruct((bsz, seq - row_off * tm, d), F32),
        compiler_params=_params("parallel", "parallel"),
        name="merge",
    )(xs, mods, ys, yg, yd, *consts)


def _ffn_kernel(x_ref, mod_ref, wi_ref, wo_ref, lg_ref, lb_ref, o_ref, *, n_ctx_rows):
    x = x_ref[...]
    h = _modulated(x_ref, mod_ref, 3, n_ctx_rows)
    acc = None
    for c0 in range(0, FFN_HIDDEN, FFN_CHUNK):
        a = _dot(h, wi_ref[:, c0:c0 + FFN_CHUNK])
        b = _dot(h, wi_ref[:, FFN_HIDDEN + c0:FFN_HIDDEN + c0 + FFN_CHUNK])
        u = (_silu(a) * b).astype(BF16)
        part = _dot(u, wo_ref[c0:c0 + FFN_CHUNK, :])
        acc = part if acc is None else acc + part
    r = DEEPNORM_ALPHA * x + _mod_row(mod_ref, 5, n_ctx_rows, x.shape[0]) * acc
    o_ref[...] = _layer_norm(r, lg_ref[...], lb_ref[...])


def _ffn_call(xs, mods, lw, n_ctx_rows, tm):
    bsz, seq, d = xs.shape
    tok = pl.BlockSpec((None, tm, d), lambda b, t: (b, t, 0))
    consts = [lw["ffn_w_in"], lw["ffn_w_out"], lw["ln2_g"], lw["ln2_b"]]
    return pl.pallas_call(
        functools.partial(_ffn_kernel, n_ctx_rows=n_ctx_rows),
        grid=(bsz, seq // tm),
        in_specs=[tok, _mod_spec(d)]
                 + [_const_spec(a.shape) for a in consts],
        out_specs=tok,
        out_shape=jax.ShapeDtypeStruct((bsz, seq, d), F32),
        compiler_params=_params("parallel", "parallel"),
        name="ffn",
    )(xs, mods, *consts)


def _rope_tables(ctx_len, lat_len, head_dim, half_sign_period):
    t = jnp.arange(lat_len, dtype=jnp.int32)
    d_axis = head_dim // 2
    inv_freq = ROPE_THETA ** (-jnp.arange(0, d_axis, 2, dtype=F32) / d_axis)
    ang_r = (t // GRID_W).astype(F32)[:, None] * inv_freq
    ang_c = (t % GRID_W).astype(F32)[:, None] * inv_freq
    ang = jnp.concatenate([ang_r, ang_r, ang_c, ang_c], axis=-1)
    cos, sin = jnp.cos(ang), jnp.sin(ang)
    reps = LANES // head_dim
    cos, sin = jnp.tile(cos, (1, reps)), jnp.tile(sin, (1, reps))
    lane = jnp.arange(LANES)
    sign = jnp.where((lane % half_sign_period) < half_sign_period // 2, -1.0, 1.0).astype(F32)
    cos = jnp.concatenate([jnp.ones((ctx_len, LANES), F32), cos], axis=0)
    sin = jnp.concatenate([jnp.zeros((ctx_len, LANES), F32), sin * sign], axis=0)
    return cos, sin


def _group_rows(v, width):
    return v.reshape(SSM_GROUPS, 1, width)


def _ssm_params(dt_bias, a_log, d_skip, norm_w):
    gh, ng = SSM_GROUP_HEADS, SSM_GROUPS

    def per_dir(v):
        r = jnp.transpose(v.reshape(2, ng, gh), (1, 0, 2)).reshape(ng, 1, 2 * gh)
        return jnp.pad(r, ((0, 0), (0, 0), (0, DT_WIDTH - 2 * gh)))

    return {
        "dtb": per_dir(dt_bias), "alog": per_dir(a_log),
        "dskip": _group_rows(jnp.repeat(d_skip, SSM_HEAD_DIM), SSM_GROUP_WIDTH),
        "nw": _group_rows(norm_w, SSM_GROUP_WIDTH),
    }


def kernel(x, c, ctx, c_ctx, ada_w, ada_b, w_in, b_gate, ssm_conv_w, ssm_conv_b, ssm_dt_bias,
           ssm_a_log, ssm_d, ssm_norm_w, w_ssm_out, gqa_q_norm, gqa_k_norm, w_gqa_out, diff_lambda,
           diff_norm_w, w_diff_out, w_o, ln1_g, ln1_b, ffn_w_in, ffn_w_out, ln2_g, ln2_b):
    bsz, lat_len, d = x.shape
    ctx_len = ctx.shape[1]
    depth = w_in.shape[0]
    assert d == D_MODEL and depth == DEPTH and w_in.shape[2] == IN_WIDTH
    assert lat_len % GRID_W == 0 and lat_len % SSM_CHUNK == 0 and ctx_len % SSM_CHUNK == 0
    tq = math.gcd(256, ctx_len)
    seq = ctx_len + lat_len
    tm = _row_tile(seq, tq)
    tm_lat = _row_tile(lat_len, tq)
    tm_ssm = _row_tile(seq, SSM_CHUNK, SSM_ROW_TILE)
    assert (lat_len // tq) % 2 == 0

    rows = -(-(bsz + 1) // SUBLANES) * SUBLANES
    cvec = jnp.concatenate([c, c_ctx[None], jnp.zeros((rows - bsz - 1, d), F32)], axis=0)
    ada = _ada_call(cvec, ada_w, ada_b)

    rope_g = _rope_tables(ctx_len, lat_len, GQA_HEAD_DIM, GQA_HEAD_DIM // 2)
    rope_d = _rope_tables(ctx_len, lat_len, DIFF_HEAD_DIM, DIFF_HEAD_DIM // 2)
    tabs = (*rope_g, *rope_d)

    xs = jnp.concatenate([ctx, x], axis=1)
    pad = jnp.zeros((d, DT_WIDTH - 2 * SSM_HEADS), BF16)
    for i in range(depth):
        need_ctx = i < depth - 1
        lam_init = 0.8 - 0.6 * math.exp(-0.3 * i)
        mod_l = ada[i, :bsz].reshape(bsz, 1, 6, d)
        mod_c = jnp.broadcast_to(ada[i, bsz].reshape(1, 1, 6, d), (bsz, 1, 6, d))
        mods = jnp.concatenate([mod_c, mod_l], axis=1)
        w = w_in[i].astype(BF16)
        w_ssm = jnp.concatenate([w[:, :OFF_GQ], pad], axis=1)
        w_attn = w[:, OFF_GQ:OFF_GATE]
        lw = {
            "w_gate": w[:, OFF_GATE:], "b_gate": b_gate[i][None],
            "w_ssm_out": w_ssm_out[i].astype(BF16), "w_gqa_out": w_gqa_out[i].astype(BF16),
            "w_diff_out": w_diff_out[i].astype(BF16), "w_o": w_o[i].astype(BF16),
            "ln1_g": ln1_g[i][None], "ln1_b": ln1_b[i][None],
            "ffn_w_in": ffn_w_in[i].astype(BF16), "ffn_w_out": ffn_w_out[i].astype(BF16),
            "ln2_g": ln2_g[i][None], "ln2_b": ln2_b[i][None],
        }
        sp = _ssm_params(ssm_dt_bias[i], ssm_a_log[i], ssm_d[i], ssm_norm_w[i])
        conv_wb = jnp.concatenate([ssm_conv_w[i], ssm_conv_b[i][None],
                                   jnp.zeros((SUBLANES - SSM_CONV - 1, SSM_CONV_DIM), F32)], axis=0)

        z, xc, bc, cc, dt = _inproj_ssm_call(xs, mods, w_ssm, conv_wb, ctx_len, tm_ssm)
        gq, gk, gvt, dq, dk, dv = _inproj_attn_call(
            xs, mods, w_attn, gqa_q_norm[i][None], gqa_k_norm[i][None], tabs, ctx_len, tm)
        y_ssm = _ssd_call(xc, bc, cc, dt, z, sp, ctx_len)
        y_gqa = _gqa_call(gq, gk, gvt, ctx_len, tq, not need_ctx)
        y_diff = _diff_call(dq, dk, dv, diff_lambda[i], diff_norm_w[i][None], lam_init, ctx_len, tq,
                            not need_ctx)
        if need_ctx:
            x1 = _merge_call(xs, mods, y_ssm, y_gqa, y_diff, lw, ctx_len, tm, 0)
            xs = _ffn_call(x1, mods, lw, ctx_len, tm)
        else:
            x1 = _merge_call(xs, mods, y_ssm, y_gqa, y_diff, lw, 0, tq, ctx_len // tq)
            xs = _ffn_call(x1, mods, lw, 0, tm_lat)
    return xs
```
